```python
import jax, jax.numpy as jnp
from jax import lax
import numpy as np

D_MODEL = 1024
BATCH = 4
SEQ = 4096
DEPTH = 4
DEC_BATCH = 128
DEC_SEQ = 4
PAST_LEN = 2048
PAGE_SIZE = 128

N_MIXERS = 3
N_LRU_LAYERS = (DEPTH + 2) // 3
N_CONV_LAYERS = (DEPTH + 1) // 3
N_ATTN_LAYERS = DEPTH // 3

D_RNN = ((4 * D_MODEL // 3 + 127) // 128) * 128
LRU_BLOCKS = 16
LRU_BLOCK_DIM = D_RNN // LRU_BLOCKS
LRU_CONV_W = 4
LRU_C = 8.0

D_CONV = D_MODEL
CM_KERNEL = 31

ATT_GROUPS = ((128, 1), (512, 4), (2048, 16))
N_GROUPS = len(ATT_GROUPS)
ATT_HEADS = 8
HEAD_DIM = D_MODEL // ATT_HEADS
ATT_BLOCK = 128
ROPE_THETA = 10000.0

D_FF = 4 * D_MODEL
EPS = 1e-6

kernel_name = 'hybrid_rglru_conformer_dilated_swa_step'


def rms_norm(x, g):
    xf = x.astype(jnp.float32)
    y = xf * lax.rsqrt(jnp.mean(xf * xf, axis=-1, keepdims=True) + EPS)
    return (y * g.astype(jnp.float32)).astype(x.dtype)


def layer_norm(x, g, b):
    xf = x.astype(jnp.float32)
    mu = jnp.mean(xf, axis=-1, keepdims=True)
    var = jnp.mean(jnp.square(xf - mu), axis=-1, keepdims=True)
    return ((xf - mu) * lax.rsqrt(var + EPS) * g.astype(jnp.float32) + b.astype(jnp.float32)).astype(x.dtype)


def causal_dwconv(x, prev, w, b):
    x_ext = jnp.concatenate([prev.astype(x.dtype), x], axis=1)
    y = lax.conv_general_dilated(x_ext, w[:, None, :].astype(x.dtype), window_strides=(1,), padding='VALID',
                                 dimension_numbers=('NWC', 'WIO', 'NWC'), feature_group_count=x.shape[-1])
    return y + b, x_ext[:, x_ext.shape[1] - (w.shape[0] - 1):]


def _lru_combine(c1, c2):
    a1, b1 = c1
    a2, b2 = c2
    return a1 * a2, a2 * b1 + b2


def rglru_mixer(x, conv_prev, h_prev, w_in, conv_w, conv_b, ga_w, ga_b, gx_w, gx_b, lam, w_out):
    B, T, _ = x.shape
    gate, u = jnp.split(x @ w_in, 2, axis=-1)
    u, conv_state = causal_dwconv(u, conv_prev, conv_w, conv_b)
    ub = u.reshape(B, T, LRU_BLOCKS, LRU_BLOCK_DIM)
    r = jax.nn.sigmoid(jnp.einsum('btnc,ncd->btnd', ub, ga_w).reshape(B, T, D_RNN) + ga_b)
    i = jax.nn.sigmoid(jnp.einsum('btnc,ncd->btnd', ub, gx_w).reshape(B, T, D_RNN) + gx_b)
    log_a = -LRU_C * r.astype(jnp.float32) * jax.nn.softplus(-lam.astype(jnp.float32))
    a = jnp.exp(log_a)
    bterm = jnp.sqrt(-jnp.expm1(2.0 * log_a)) * (i * u).astype(jnp.float32)
    bterm = bterm.at[:, 0].add(a[:, 0] * h_prev.astype(jnp.float32))
    _, h = lax.associative_scan(_lru_combine, (a, bterm), axis=1)
    y = (h.astype(x.dtype) * jax.nn.gelu(gate)) @ w_out
    return y, conv_state, h[:, -1].astype(h_prev.dtype)


def conformer_conv_mixer(x, conv_prev, w_pw1, b_pw1, dw_w, dw_b, ln_g, ln_b, w_pw2, b_pw2):
    a, g = jnp.split(x @ w_pw1 + b_pw1, 2, axis=-1)
    u = a * jax.nn.sigmoid(g)
    u, conv_state = causal_dwconv(u, conv_prev, dw_w, dw_b)
    u = layer_norm(u, ln_g, ln_b)
    return jax.nn.silu(u) @ w_pw2 + b_pw2, conv_state


def ffn(x, w1, w2):
    return jnp.square(jax.nn.relu(x @ w1)) @ w2


def rope(x, pos):
    half = HEAD_DIM // 2
    inv_freq = ROPE_THETA ** (-jnp.arange(half, dtype=jnp.float32) / half)
    ang = pos.astype(jnp.float32)[:, None] * inv_freq
    cos = jnp.cos(ang)[None, :, None, None, :]
    sin = jnp.sin(ang)[None, :, None, None, :]
    xf = x.astype(jnp.float32)
    x1, x2 = xf[..., :half], xf[..., half:]
    return jnp.concatenate([x1 * cos - x2 * sin, x2 * cos + x1 * sin], axis=-1).astype(x.dtype)


def qkv_groups(x, w_qkv, pos):
    B, T, _ = x.shape
    qkv = (x @ w_qkv).reshape(B, T, N_GROUPS, 3, ATT_HEADS, HEAD_DIM)
    return rope(qkv[:, :, :, 0], pos), rope(qkv[:, :, :, 1], pos), qkv[:, :, :, 2]


def dilated_attn_prompt(q, k, v, dil, n_back):
    B, T, H, hd = q.shape
    span = dil * ATT_BLOCK
    Tp = -(-T // span) * span
    nb = Tp // span

    def to_blocks(z):
        z = jnp.pad(z, ((0, 0), (0, Tp - T), (0, 0), (0, 0)))
        z = z.reshape(B, Tp // dil, dil, H, hd).transpose(0, 2, 1, 3, 4)
        return z.reshape(B, dil, nb, ATT_BLOCK, H, hd)

    def with_prev(z):
        prev = jnp.pad(z[:, :, :-1], ((0, 0), (0, 0), (1, 0), (0, 0), (0, 0), (0, 0)))
        return jnp.concatenate([prev, z], axis=3)

    qb = to_blocks(q)
    kk = with_prev(to_blocks(k))
    vv = with_prev(to_blocks(v))
    s = jnp.einsum('brnqhd,brnkhd->brnhqk', qb, kk, preferred_element_type=jnp.float32) * (HEAD_DIM ** -0.5)
    qi = jnp.arange(ATT_BLOCK)[:, None]
    kj = jnp.arange(2 * ATT_BLOCK)[None, :]
    dist = qi + ATT_BLOCK - kj
    band = (dist >= 0) & (dist <= n_back)
    valid = band[None] & ((jnp.arange(nb)[:, None, None] > 0) | (kj >= ATT_BLOCK)[None])
    s = jnp.where(valid[None, None, :, None], s, -jnp.inf)
    lse = jax.nn.logsumexp(s, axis=-1)
    p = jnp.exp(s - lse[..., None])
    o = jnp.einsum('brnhqk,brnkhd->brnqhd', p.astype(v.dtype), vv)
    o = o.reshape(B, dil, Tp // dil, H, hd).transpose(0, 2, 1, 3, 4).reshape(B, Tp, H, hd)[:, :T]
    lse = lse.transpose(0, 1, 2, 4, 3).reshape(B, dil, Tp // dil, H).transpose(0, 2, 1, 3).reshape(B, Tp, H)[:, :T]
    return o, lse


def dilated_attn_sample(q, k_ext, v_ext, dil, n_back, w_buf):
    S = q.shape[1]
    idx = w_buf + jnp.arange(S)[:, None] - dil * jnp.arange(n_back + 1)[None, :]
    valid = idx >= 0
    idx = jnp.maximum(idx, 0)
    kg = k_ext[:, idx]
    vg = v_ext[:, idx]
    s = jnp.einsum('bqhd,bqkhd->bqhk', q, kg, preferred_element_type=jnp.float32) * (HEAD_DIM ** -0.5)
    s = jnp.where(valid[None, :, None, :], s, -jnp.inf)
    lse = jax.nn.logsumexp(s, axis=-1)
    p = jnp.exp(s - lse[..., None])
    o = jnp.einsum('bqhk,bqkhd->bqhd', p.astype(v_ext.dtype), vg)
    return o, lse


def merge_groups(outs, lses, dtype):
    w = jax.nn.softmax(jnp.stack(lses, axis=0), axis=0)
    o = jnp.einsum('gbth,gbthd->bthd', w, jnp.stack(outs, axis=0).astype(jnp.float32))
    return o.reshape(o.shape[0], o.shape[1], -1).astype(dtype)


def attn_prompt(x, w_qkv, w_o):
    B, T, _ = x.shape
    q, k, v = qkv_groups(x, w_qkv, jnp.arange(T))
    outs, lses, kv_state = [], [], []
    for g, (win, dil) in enumerate(ATT_GROUPS):
        o, l = dilated_attn_prompt(q[:, :, g], k[:, :, g], v[:, :, g], dil, win // dil)
        outs.append(o)
        lses.append(l)
        keep = min(win, T)
        kv_state.append(jnp.stack([k[:, T - keep:, g], v[:, T - keep:, g]], axis=2))
    return merge_groups(outs, lses, x.dtype) @ w_o, kv_state


def attn_sample(x, kv_bufs, w_qkv, w_o):
    S = x.shape[1]
    q, k, v = qkv_groups(x, w_qkv, PAST_LEN + jnp.arange(S))
    outs, lses, kv_new = [], [], []
    for g, (win, dil) in enumerate(ATT_GROUPS):
        buf = kv_bufs[g]
        k_ext = jnp.concatenate([buf[:, :, 0].astype(k.dtype), k[:, :, g]], axis=1)
        v_ext = jnp.concatenate([buf[:, :, 1].astype(v.dtype), v[:, :, g]], axis=1)
        o, l = dilated_attn_sample(q[:, :, g], k_ext, v_ext, dil, win // dil, buf.shape[1])
        outs.append(o)
        lses.append(l)
        kv_new.append(jnp.stack([k[:, :, g], v[:, :, g]], axis=2).astype(buf.dtype))
    return merge_groups(outs, lses, x.dtype) @ w_o, kv_new


def setup_inputs(seed: int = 0) -> dict:
    key = jax.random.key(seed)
    ks = iter(jax.random.split(key, 40))

    def nrm(shape, scale=1.0):
        return scale * jax.random.normal(next(ks), shape, jnp.float32)

    inp = {}
    inp['x_prompt'] = nrm((BATCH, SEQ, D_MODEL))
    inp['x_sample'] = nrm((DEC_BATCH, DEC_SEQ, D_MODEL))
    inp['state_lru_conv'] = nrm((N_LRU_LAYERS, DEC_BATCH, LRU_CONV_W - 1, D_RNN))
    inp['state_lru_h'] = nrm((N_LRU_LAYERS, DEC_BATCH, D_RNN), 0.5)
    inp['state_cm_conv'] = nrm((N_CONV_LAYERS, DEC_BATCH, CM_KERNEL - 1, D_CONV), 0.5)
    inp['cache_kv_w128'] = nrm((N_ATTN_LAYERS, DEC_BATCH, min(ATT_GROUPS[0][0], PAST_LEN), 2, ATT_HEADS, HEAD_DIM))
    inp['cache_kv_w512'] = nrm((N_ATTN_LAYERS, DEC_BATCH, min(ATT_GROUPS[1][0], PAST_LEN), 2, ATT_HEADS, HEAD_DIM))
    inp['cache_kv_w2048'] = nrm((N_ATTN_LAYERS, DEC_BATCH, min(ATT_GROUPS[2][0], PAST_LEN), 2, ATT_HEADS, HEAD_DIM))
    inp['norm_mix'] = 1.0 + nrm((DEPTH, D_MODEL), 0.02)
    inp['norm_ffn'] = 1.0 + nrm((DEPTH, D_MODEL), 0.02)
    inp['norm_final'] = 1.0 + nrm((D_MODEL,), 0.02)
    inp['lru_w_in'] = nrm((N_LRU_LAYERS, D_MODEL, 2 * D_RNN), D_MODEL ** -0.5)
    inp['lru_conv_w'] = nrm((N_LRU_LAYERS, LRU_CONV_W, D_RNN), LRU_CONV_W ** -0.5)
    inp['lru_conv_b'] = nrm((N_LRU_LAYERS, D_RNN), 0.01)
    inp['lru_gate_a_w'] = nrm((N_LRU_LAYERS, LRU_BLOCKS, LRU_BLOCK_DIM, LRU_BLOCK_DIM), LRU_BLOCK_DIM ** -0.5)
    inp['lru_gate_a_b'] = nrm((N_LRU_LAYERS, D_RNN), 0.01)
    inp['lru_gate_x_w'] = nrm((N_LRU_LAYERS, LRU_BLOCKS, LRU_BLOCK_DIM, LRU_BLOCK_DIM), LRU_BLOCK_DIM ** -0.5)
    inp['lru_gate_x_b'] = nrm((N_LRU_LAYERS, D_RNN), 0.01)
    a0 = jax.random.uniform(next(ks), (N_LRU_LAYERS, D_RNN), jnp.float32, 0.9, 0.999)
    s0 = a0 ** (1.0 / LRU_C)
    inp['lru_lambda'] = jnp.log(s0) - jnp.log1p(-s0)
    inp['lru_w_out'] = nrm((N_LRU_LAYERS, D_RNN, D_MODEL), D_RNN ** -0.5)
    inp['cm_w_pw1'] = nrm((N_CONV_LAYERS, D_MODEL, 2 * D_CONV), D_MODEL ** -0.5)
    inp['cm_b_pw1'] = nrm((N_CONV_LAYERS, 2 * D_CONV), 0.01)
    inp['cm_dw_w'] = nrm((N_CONV_LAYERS, CM_KERNEL, D_CONV), CM_KERNEL ** -0.5)
    inp['cm_dw_b'] = nrm((N_CONV_LAYERS, D_CONV), 0.01)
    inp['cm_ln_g'] = 1.0 + nrm((N_CONV_LAYERS, D_CONV), 0.02)
    inp['cm_ln_b'] = nrm((N_CONV_LAYERS, D_CONV), 0.01)
    inp['cm_w_pw2'] = nrm((N_CONV_LAYERS, D_CONV, D_MODEL), D_CONV ** -0.5)
    inp['cm_b_pw2'] = nrm((N_CONV_LAYERS, D_MODEL), 0.01)
    inp['att_w_qkv'] = nrm((N_ATTN_LAYERS, D_MODEL, N_GROUPS * 3 * ATT_HEADS * HEAD_DIM), D_MODEL ** -0.5)
    inp['att_w_o'] = nrm((N_ATTN_LAYERS, ATT_HEADS * HEAD_DIM, D_MODEL), (ATT_HEADS * HEAD_DIM) ** -0.5)
    inp['ffn_w1'] = nrm((DEPTH, D_MODEL, D_FF), D_MODEL ** -0.5)
    inp['ffn_w2'] = nrm((DEPTH, D_FF, D_MODEL), D_FF ** -0.5)
    return inp


def reference(x_prompt, x_sample, state_lru_conv, state_lru_h, state_cm_conv,
              cache_kv_w128, cache_kv_w512, cache_kv_w2048,
              norm_mix, norm_ffn, norm_final,
              lru_w_in, lru_conv_w, lru_conv_b, lru_gate_a_w, lru_gate_a_b,
              lru_gate_x_w, lru_gate_x_b, lru_lambda, lru_w_out,
              cm_w_pw1, cm_b_pw1, cm_dw_w, cm_dw_b, cm_ln_g, cm_ln_b, cm_w_pw2, cm_b_pw2,
              att_w_qkv, att_w_o, ffn_w1, ffn_w2):
    xp, xs = x_prompt, x_sample
    B = xp.shape[0]
    kv_caches = (cache_kv_w128, cache_kv_w512, cache_kv_w2048)
    p_lru_conv, p_lru_h, s_lru_conv, s_lru_h = [], [], [], []
    p_cm_conv, s_cm_conv = [], []
    p_kv = [[] for _ in ATT_GROUPS]
    s_kv = [[] for _ in ATT_GROUPS]
    for i in range(DEPTH):
        m, j = i % N_MIXERS, i // N_MIXERS
        up, us = rms_norm(xp, norm_mix[i]), rms_norm(xs, norm_mix[i])
        if m == 0:
            lp = (lru_w_in[j], lru_conv_w[j], lru_conv_b[j], lru_gate_a_w[j], lru_gate_a_b[j],
                  lru_gate_x_w[j], lru_gate_x_b[j], lru_lambda[j], lru_w_out[j])
            yp, cp, hp = rglru_mixer(up, jnp.zeros((B, LRU_CONV_W - 1, D_RNN), xp.dtype),
                                     jnp.zeros((B, D_RNN), state_lru_h.dtype), *lp)
            ys, cs, hs = rglru_mixer(us, state_lru_conv[j], state_lru_h[j], *lp)
            p_lru_conv.append(cp)
            p_lru_h.append(hp)
            s_lru_conv.append(cs.astype(state_lru_conv.dtype))
            s_lru_h.append(hs)
        elif m == 1:
            cpar = (cm_w_pw1[j], cm_b_pw1[j], cm_dw_w[j], cm_dw_b[j], cm_ln_g[j], cm_ln_b[j], cm_w_pw2[j], cm_b_pw2[j])
            yp, cp = conformer_conv_mixer(up, jnp.zeros((B, CM_KERNEL - 1, D_CONV), xp.dtype), *cpar)
            ys, cs = conformer_conv_mixer(us, state_cm_conv[j], *cpar)
            p_cm_conv.append(cp)
            s_cm_conv.append(cs.astype(state_cm_conv.dtype))
        else:
            yp, kvp = attn_prompt(up, att_w_qkv[j], att_w_o[j])
            ys, kvs = attn_sample(us, tuple(c[j] for c in kv_caches), att_w_qkv[j], att_w_o[j])
            for g in range(N_GROUPS):
                p_kv[g].append(kvp[g])
                s_kv[g].append(kvs[g])
        xp = xp + yp
        xs = xs + ys
        xp = xp + ffn(rms_norm(xp, norm_ffn[i]), ffn_w1[i], ffn_w2[i])
        xs = xs + ffn(rms_norm(xs, norm_ffn[i]), ffn_w1[i], ffn_w2[i])
    y_prompt = rms_norm(xp, norm_final)
    y_sample = rms_norm(xs, norm_final)
    return (y_prompt, y_sample,
            jnp.stack(p_lru_conv), jnp.stack(p_lru_h), jnp.stack(p_cm_conv),
            jnp.stack(p_kv[0]), jnp.stack(p_kv[1]), jnp.stack(p_kv[2]),
            jnp.stack(s_lru_conv), jnp.stack(s_lru_h), jnp.stack(s_cm_conv),
            jnp.stack(s_kv[0]), jnp.stack(s_kv[1]), jnp.stack(s_kv[2]))
```

```python
import functools

import numpy as np
import jax
import jax.numpy as jnp
from jax import lax
from jax.experimental import pallas as pl
from jax.experimental.pallas import tpu as pltpu

F32 = jnp.float32
BF16 = jnp.bfloat16

LANES = 128
D_MODEL = 1024
D_RNN = 1408
LRU_BLOCKS = 16
LRU_BLOCK_DIM = D_RNN // LRU_BLOCKS
LRU_CONV_W = 4
LRU_C = 8.0
CM_KERNEL = 31
ATT_GROUPS = ((128, 1), (512, 4), (2048, 16))
N_GROUPS = 3
ATT_HEADS = 8
HEAD_DIM = 128
ATT_BLOCK = 128
ROPE_THETA = 10000.0
PAST_LEN = 2048
EPS = 1e-6
QKV_COLS = N_GROUPS * 3 * ATT_HEADS * HEAD_DIM
GROUP_COLS = 3 * ATT_HEADS * HEAD_DIM
HD_ALL = ATT_HEADS * HEAD_DIM
O_COLS = HD_ALL + LANES
N_CT = D_RNN // LANES
SCALE = HEAD_DIM ** -0.5
MIB = 1024 * 1024


def _cparams(sem, vmem_mib):
    return pltpu.CompilerParams(dimension_semantics=sem, vmem_limit_bytes=vmem_mib * MIB)


def _rms(x, g):
    ms = jnp.mean(x * x, axis=-1, keepdims=True)
    return (x * lax.rsqrt(ms + EPS)) * g


def _sigmoid(x):
    return 1.0 / (1.0 + jnp.exp(-x))


def _softplus(x):
    return jnp.maximum(x, 0.0) + jnp.log1p(jnp.exp(-jnp.abs(x)))


def _gelu_tanh(x):
    return x * (0.5 * (1.0 + jnp.tanh(0.7978845608028654 * (x + 0.044715 * (x * x * x)))))


def _norm_proj_kernel(x_ref, g_ref, w_ref, b_ref, o_ref, xn_ref):
    @pl.when(pl.program_id(1) == 0)
    def _():
        xn_ref[...] = _rms(x_ref[...], g_ref[...]).astype(BF16)

    acc = jnp.dot(xn_ref[...], w_ref[...], preferred_element_type=F32)
    o_ref[...] = (acc + b_ref[...]).astype(o_ref.dtype)


def norm_proj(x, g, w, b, *, tm, tn, out_dtype=F32):
    n, d = x.shape
    nout = w.shape[1]
    return pl.pallas_call(
        _norm_proj_kernel,
        grid=(n // tm, nout // tn),
        in_specs=[
            pl.BlockSpec((tm, d), lambda i, j: (i, 0)),
            pl.BlockSpec((1, d), lambda i, j: (0, 0)),
            pl.BlockSpec((d, tn), lambda i, j: (0, j)),
            pl.BlockSpec((1, tn), lambda i, j: (0, j)),
        ],
        out_specs=pl.BlockSpec((tm, tn), lambda i, j: (i, j)),
        out_shape=jax.ShapeDtypeStruct((n, nout), out_dtype),
        scratch_shapes=[pltpu.VMEM((tm, d), BF16)],
        compiler_params=_cparams(("parallel", "arbitrary"), 48),
        name="norm_proj",
    )(x, g.reshape(1, d), w, b.reshape(1, nout))


def _norm_qkv_kernel(x_ref, g_ref, w_ref, cos_ref, sin_ref, o_ref, xn_ref):
    j = pl.program_id(1)

    @pl.when(j == 0)
    def _():
        xn_ref[...] = _rms(x_ref[...], g_ref[...]).astype(BF16)

    acc = jnp.dot(xn_ref[...], w_ref[...], preferred_element_type=F32)
    which = j % 3

    @pl.when(which == 2)
    def _():
        o_ref[...] = acc.astype(o_ref.dtype)

    @pl.when(which < 2)
    def _():
        cos = cos_ref[...]
        sin = sin_ref[...]
        for h in range(ATT_HEADS):
            xh = acc[:, h * HEAD_DIM:(h + 1) * HEAD_DIM]
            rot = pltpu.roll(xh, HEAD_DIM // 2, 1)
            o_ref[:, h * HEAD_DIM:(h + 1) * HEAD_DIM] = (xh * cos + rot * sin).astype(o_ref.dtype)


def norm_qkv(x, g, w, cos, sin, *, tm, out_dtype):
    n, d = x.shape
    period = cos.shape[0]
    nper = period // tm
    return pl.pallas_call(
        _norm_qkv_kernel,
        grid=(n // tm, QKV_COLS // HD_ALL),
        in_specs=[
            pl.BlockSpec((tm, d), lambda i, j: (i, 0)),
            pl.BlockSpec((1, d), lambda i, j: (0, 0)),
            pl.BlockSpec((d, HD_ALL), lambda i, j: (0, j)),
            pl.BlockSpec((tm, HEAD_DIM), lambda i, j: (i % nper, 0)),
            pl.BlockSpec((tm, HEAD_DIM), lambda i, j: (i % nper, 0)),
        ],
        out_specs=pl.BlockSpec((tm, HD_ALL), lambda i, j: (i, j)),
        out_shape=jax.ShapeDtypeStruct((n, QKV_COLS), out_dtype),
        scratch_shapes=[pltpu.VMEM((tm, d), BF16)],
        compiler_params=_cparams(("parallel", "arbitrary"), 48),
        name="norm_qkv",
    )(x, g.reshape(1, d), w, cos, sin)


def _ffn_kernel(x_ref, g_ref, w1_ref, w2_ref, gf_ref, o_ref, xn_ref, acc_ref, *, final):
    f = pl.program_id(1)

    @pl.when(f == 0)
    def _():
        xn_ref[...] = _rms(x_ref[...], g_ref[...]).astype(BF16)
        acc_ref[...] = jnp.zeros_like(acc_ref)

    h = jnp.dot(xn_ref[...], w1_ref[...], preferred_element_type=F32)
    h = jnp.maximum(h, 0.0)
    h = (h * h).astype(BF16)
    acc_ref[...] += jnp.dot(h, w2_ref[...], preferred_element_type=F32)

    @pl.when(f == pl.num_programs(1) - 1)
    def _():
        y = x_ref[...] + acc_ref[...]
        if final:
            y = _rms(y, gf_ref[...])
        o_ref[...] = y


def ffn(x, g, w1, w2, gf, *, tm, tf, final):
    n, d = x.shape
    dff = w1.shape[1]
    return pl.pallas_call(
        functools.partial(_ffn_kernel, final=final),
        grid=(n // tm, dff // tf),
        in_specs=[
            pl.BlockSpec((tm, d), lambda i, f: (i, 0)),
            pl.BlockSpec((1, d), lambda i, f: (0, 0)),
            pl.BlockSpec((d, tf), lambda i, f: (0, f)),
            pl.BlockSpec((tf, d), lambda i, f: (f, 0)),
            pl.BlockSpec((1, d), lambda i, f: (0, 0)),
        ],
        out_specs=pl.BlockSpec((tm, d), lambda i, f: (i, 0)),
        out_shape=jax.ShapeDtypeStruct((n, d), F32),
        scratch_shapes=[pltpu.VMEM((tm, d), BF16), pltpu.VMEM((tm, d), F32)],
        compiler_params=_cparams(("parallel", "arbitrary"), 48),
        name="ffn",
    )(x, g.reshape(1, d), w1, w2, gf.reshape(1, d))


def _out_proj_kernel(a_ref, w_ref, b_ref, r_ref, o_ref):
    y = jnp.dot(a_ref[...].astype(BF16), w_ref[...], preferred_element_type=F32)
    o_ref[...] = r_ref[...] + (y + b_ref[...])


def out_proj(a, w, b, resid, *, tm):
    n, k = a.shape
    d = w.shape[1]
    return pl.pallas_call(
        _out_proj_kernel,
        grid=(n // tm,),
        in_specs=[
            pl.BlockSpec((tm, k), lambda i: (i, 0)),
            pl.BlockSpec((k, d), lambda i: (0, 0)),
            pl.BlockSpec((1, d), lambda i: (0, 0)),
            pl.BlockSpec((tm, d), lambda i: (i, 0)),
        ],
        out_specs=pl.BlockSpec((tm, d), lambda i: (i, 0)),
        out_shape=jax.ShapeDtypeStruct((n, d), F32),
        compiler_params=_cparams(("parallel",), 48),
        name="out_proj",
    )(a, w, b.reshape(1, d), resid)


def _merge_out_proj_kernel(o0_ref, o1_ref, o2_ref, w_ref, r_ref, o_ref, m_scr):
    o_refs = (o0_ref, o1_ref, o2_ref)
    lses = [r[:, HD_ALL:HD_ALL + ATT_HEADS] for r in o_refs]
    mx = jnp.maximum(jnp.maximum(lses[0], lses[1]), lses[2])
    es = [jnp.exp(l - mx) for l in lses]
    inv = 1.0 / (es[0] + es[1] + es[2])
    ws = [e * inv for e in es]
    for h in range(ATT_HEADS):
        sl = slice(h * HEAD_DIM, (h + 1) * HEAD_DIM)
        acc = ws[0][:, h:h + 1] * o0_ref[:, sl]
        acc = acc + ws[1][:, h:h + 1] * o1_ref[:, sl]
        acc = acc + ws[2][:, h:h + 1] * o2_ref[:, sl]
        m_scr[:, sl] = acc.astype(BF16)
    o_ref[...] = r_ref[...] + jnp.dot(m_scr[...], w_ref[...], preferred_element_type=F32)


def merge_out_proj(o0, o1, o2, w, resid, *, tm):
    n = o0.shape[0]
    d = w.shape[1]
    ospec = pl.BlockSpec((tm, O_COLS), lambda i: (i, 0))
    return pl.pallas_call(
        _merge_out_proj_kernel,
        grid=(n // tm,),
        in_specs=[ospec, ospec, ospec,
                  pl.BlockSpec((HD_ALL, d), lambda i: (0, 0)),
                  pl.BlockSpec((tm, d), lambda i: (i, 0))],
        out_specs=pl.BlockSpec((tm, d), lambda i: (i, 0)),
        out_shape=jax.ShapeDtypeStruct((n, d), F32),
        scratch_shapes=[pltpu.VMEM((tm, HD_ALL), BF16)],
        compiler_params=_cparams(("parallel",), 48),
        name="merge_out_proj",
    )(o0, o1, o2, w, resid)


def _band_slices(c):
    lo = max(c - 1, 0)
    hi = min(c + 2, N_CT)
    return (lo * LANES, hi * LANES), ((lo - (c - 1)) * LANES, (hi - (c - 1)) * LANES)


def _lru_gates(xg_tile_fn, wb_ref, c, u, gab, gxb, sp):
    (xlo, xhi), (rlo, rhi) = _band_slices(c)
    z = jnp.dot(xg_tile_fn(xlo, xhi), wb_ref[c, rlo:rhi, :], preferred_element_type=F32)
    r = _sigmoid(z[:, :LANES] + gab)
    i = _sigmoid(z[:, LANES:] + gxb)
    log_a = (-LRU_C * r) * sp
    a = jnp.exp(log_a)
    b = jnp.sqrt(-jnp.tanh(log_a) * (a * a + 1.0)) * (i * u)
    return a, b


def _lru_mix_prompt_kernel(gu_ref, cw_ref, cb_ref, wb_ref, gab_ref, gxb_ref, lam_ref,
                           hg_ref, cs_ref, hl_ref,
                           ubuf, cv_scr, xg_scr, a_scr, b_scr, hc_scr, *, tt):
    t = pl.program_id(1)
    nt = pl.num_programs(1)
    C = D_RNN
    RB = 64
    L = tt // 8

    @pl.when(t == 0)
    def _():
        ubuf[0:8, :] = jnp.zeros((8, C), F32)
        hc_scr[...] = jnp.zeros((1, C), F32)

    ubuf[8:8 + tt, :] = gu_ref[:, C:2 * C]

    for c in range(N_CT):
        cs = slice(c * LANES, (c + 1) * LANES)
        for r0 in range(0, tt, RB):
            acc = cw_ref[0:1, cs] * ubuf[r0 + 5:r0 + 5 + RB, cs]
            for k in range(1, LRU_CONV_W):
                acc = acc + cw_ref[k:k + 1, cs] * ubuf[r0 + 5 + k:r0 + 5 + k + RB, cs]
            acc = acc + cb_ref[:, cs]
            cv_scr[r0:r0 + RB, cs] = acc
            xg_scr[r0:r0 + RB, cs] = acc.astype(BF16)

    @pl.when(t == nt - 1)
    def _():
        cs_ref[0] = ubuf[tt + 5:tt + 8, :]

    ubuf[0:8, :] = ubuf[tt:tt + 8, :]

    sp = _softplus(-lam_ref[...])
    for c in range(N_CT):
        cs = slice(c * LANES, (c + 1) * LANES)
        a, b = _lru_gates(lambda lo, hi: xg_scr[:, lo:hi], wb_ref, c, cv_scr[:, cs],
                          gab_ref[:, cs], gxb_ref[:, cs], sp[:, cs])
        a_scr[c] = a
        b_scr[c] = b

    def body(i, carry):
        out = []
        for c in range(N_CT):
            hloc, prod = carry[c]
            a = a_scr[c, pl.ds(i, 8, stride=L), :]
            b = b_scr[c, pl.ds(i, 8, stride=L), :]
            hloc = a * hloc + b
            prod = a * prod
            b_scr[c, pl.ds(i, 8, stride=L), :] = hloc
            a_scr[c, pl.ds(i, 8, stride=L), :] = prod
            out.append((hloc, prod))
        return tuple(out)

    init = tuple((jnp.zeros((8, LANES), F32), jnp.ones((8, LANES), F32)) for _ in range(N_CT))
    ends = lax.fori_loop(0, L, body, init)
    for c in range(N_CT):
        cs = slice(c * LANES, (c + 1) * LANES)
        hloc, prod = ends[c]
        hin = hc_scr[:, cs]
        for s in range(8):
            rows = slice(s * L, (s + 1) * L)
            h = b_scr[c, rows, :] + a_scr[c, rows, :] * hin
            hg_ref[rows, cs] = (h * _gelu_tanh(gu_ref[rows, cs])).astype(hg_ref.dtype)
            hin = hloc[s:s + 1, :] + prod[s:s + 1, :] * hin
        hc_scr[:, cs] = hin

    @pl.when(t == nt - 1)
    def _():
        hl_ref[0] = hc_scr[...]


def lru_mix_prompt(gu, cw, cb, wb, gab, gxb, lam, *, batch, seq, tt):
    C = D_RNN
    nt = seq // tt
    row = lambda v: v.reshape(1, C)
    full2 = lambda shape: pl.BlockSpec(shape, lambda b, t: (0,) * len(shape))
    return pl.pallas_call(
        functools.partial(_lru_mix_prompt_kernel, tt=tt),
        grid=(batch, nt),
        in_specs=[
            pl.BlockSpec((tt, 2 * C), lambda b, t: (b * nt + t, 0)),
            full2((LRU_CONV_W, C)), full2((1, C)), full2((N_CT, 3 * LANES, 2 * LANES)),
            full2((1, C)), full2((1, C)), full2((1, C)),
        ],
        out_specs=[
            pl.BlockSpec((tt, C), lambda b, t: (b * nt + t, 0)),
            pl.BlockSpec((1, LRU_CONV_W - 1, C), lambda b, t: (b, 0, 0)),
            pl.BlockSpec((1, 1, C), lambda b, t: (b, 0, 0)),
        ],
        out_shape=[
            jax.ShapeDtypeStruct((batch * seq, C), BF16),
            jax.ShapeDtypeStruct((batch, LRU_CONV_W - 1, C), F32),
            jax.ShapeDtypeStruct((batch, 1, C), F32),
        ],
        scratch_shapes=[
            pltpu.VMEM((tt + 8, C), F32), pltpu.VMEM((tt, C), F32), pltpu.VMEM((tt, C), BF16),
            pltpu.VMEM((N_CT, tt, LANES), F32), pltpu.VMEM((N_CT, tt, LANES), F32), pltpu.VMEM((1, C), F32),
        ],
        compiler_params=_cparams(("arbitrary", "arbitrary"), 48),
        name="lru_mix_prompt",
    )(gu, cw, row(cb), wb, row(gab), row(gxb), row(lam))


def _lru_mix_sample_kernel(gu_ref, cst_ref, hp_ref, cw_ref, cb_ref, wb_ref, gab_ref, gxb_ref, lam_ref,
                           hg_ref, cso_ref, hl_ref, cv_scr, xg_scr, *, db, s_len):
    del db
    C = D_RNN
    W1 = LRU_CONV_W - 1

    def ext(j, cs):
        if j < W1:
            return cst_ref[j, :, cs]
        return gu_ref[j - W1, :, C + cs.start:C + cs.stop]

    for c in range(N_CT):
        cs = slice(c * LANES, (c + 1) * LANES)
        for s in range(s_len):
            acc = cw_ref[0:1, cs] * ext(s, cs)
            for k in range(1, LRU_CONV_W):
                acc = acc + cw_ref[k:k + 1, cs] * ext(s + k, cs)
            acc = acc + cb_ref[:, cs]
            cv_scr[s, :, cs] = acc
            xg_scr[s, :, cs] = acc.astype(BF16)
        for k in range(W1):
            cso_ref[k, :, cs] = ext(s_len + k, cs)

    sp = _softplus(-lam_ref[...])
    for c in range(N_CT):
        cs = slice(c * LANES, (c + 1) * LANES)
        h = hp_ref[:, cs]
        for s in range(s_len):
            a, b = _lru_gates(lambda lo, hi: xg_scr[s, :, lo:hi], wb_ref, c, cv_scr[s, :, cs],
                              gab_ref[:, cs], gxb_ref[:, cs], sp[:, cs])
            h = a * h + b
            hg_ref[s, :, cs] = h * _gelu_tanh(gu_ref[s, :, cs])
        hl_ref[:, cs] = h


def lru_mix_sample(gu, cst, hp, cw, cb, wb, gab, gxb, lam, *, db, s_len):
    C = D_RNN
    row = lambda v: v.reshape(1, C)
    return pl.pallas_call(
        functools.partial(_lru_mix_sample_kernel, db=db, s_len=s_len),
        out_shape=[
            jax.ShapeDtypeStruct((s_len, db, C), F32),
            jax.ShapeDtypeStruct((LRU_CONV_W - 1, db, C), F32),
            jax.ShapeDtypeStruct((db, C), F32),
        ],
        scratch_shapes=[pltpu.VMEM((s_len, db, C), F32), pltpu.VMEM((s_len, db, C), BF16)],
        compiler_params=pltpu.CompilerParams(vmem_limit_bytes=48 * MIB),
        name="lru_mix_sample",
    )(gu, cst, hp, cw, row(cb), wb, row(gab), row(gxb), row(lam))


def _ln_silu(y, g, b):
    mu = jnp.mean(y, axis=-1, keepdims=True)
    yc = y - mu
    var = jnp.mean(yc * yc, axis=-1, keepdims=True)
    z = yc * lax.rsqrt(var + EPS) * g + b
    return z * _sigmoid(z)


def _cm_mix_prompt_kernel(ag_ref, dw_ref, db_ref, lg_ref, lb_ref, o_ref, cs_ref, ubuf, cv_scr, *, tt):
    t = pl.program_id(1)
    nt = pl.num_programs(1)
    C = D_MODEL
    RB = 64
    HALO = 32
    OFF = HALO - (CM_KERNEL - 1)

    @pl.when(t == 0)
    def _():
        ubuf[0:HALO, :] = jnp.zeros((HALO, C), F32)

    for r0 in range(0, tt, RB):
        a = ag_ref[r0:r0 + RB, 0:C]
        g = ag_ref[r0:r0 + RB, C:2 * C]
        ubuf[HALO + r0:HALO + r0 + RB, :] = a * _sigmoid(g)

    for c in range(C // LANES):
        cs = slice(c * LANES, (c + 1) * LANES)
        for r0 in range(0, tt, RB):
            acc = dw_ref[0:1, cs] * ubuf[r0 + OFF:r0 + OFF + RB, cs]
            for k in range(1, CM_KERNEL):
                acc = acc + dw_ref[k:k + 1, cs] * ubuf[r0 + OFF + k:r0 + OFF + k + RB, cs]
            cv_scr[r0:r0 + RB, cs] = acc + db_ref[:, cs]

    @pl.when(t == nt - 1)
    def _():
        cs_ref[0] = ubuf[tt + OFF:tt + HALO, :]

    ubuf[0:HALO, :] = ubuf[tt:tt + HALO, :]

    for r0 in range(0, tt, RB):
        o_ref[r0:r0 + RB, :] = _ln_silu(cv_scr[r0:r0 + RB, :], lg_ref[...], lb_ref[...]).astype(o_ref.dtype)


def cm_mix_prompt(ag, dw, db_, lg, lb, *, batch, seq, tt):
    C = D_MODEL
    nt = seq // tt
    row = lambda v: v.reshape(1, C)
    full2 = lambda shape: pl.BlockSpec(shape, lambda b, t: (0,) * len(shape))
    return pl.pallas_call(
        functools.partial(_cm_mix_prompt_kernel, tt=tt),
        grid=(batch, nt),
        in_specs=[
            pl.BlockSpec((tt, 2 * C), lambda b, t: (b * nt + t, 0)),
            full2((CM_KERNEL, C)), full2((1, C)), full2((1, C)), full2((1, C)),
        ],
        out_specs=[
            pl.BlockSpec((tt, C), lambda b, t: (b * nt + t, 0)),
            pl.BlockSpec((1, CM_KERNEL - 1, C), lambda b, t: (b, 0, 0)),
        ],
        out_shape=[
            jax.ShapeDtypeStruct((batch * seq, C), BF16),
            jax.ShapeDtypeStruct((batch, CM_KERNEL - 1, C), F32),
        ],
        scratch_shapes=[pltpu.VMEM((tt + 32, C), F32), pltpu.VMEM((tt, C), F32)],
        compiler_params=_cparams(("arbitrary", "arbitrary"), 48),
        name="cm_mix_prompt",
    )(ag, dw, row(db_), row(lg), row(lb))


def _cm_mix_sample_kernel(ag_ref, st_ref, dw_ref, db_ref, lg_ref, lb_ref, o_ref, so_ref, u_scr, cv_scr,
                          *, nb, s_len):
    del nb
    C = D_MODEL
    W1 = CM_KERNEL - 1

    for s in range(s_len):
        u_scr[s] = ag_ref[s, :, 0:C] * _sigmoid(ag_ref[s, :, C:2 * C])

    def ext(j, cs):
        if j < W1:
            return st_ref[j, :, cs]
        return u_scr[j - W1, :, cs]

    for c in range(C // LANES):
        cs = slice(c * LANES, (c + 1) * LANES)
        for s in range(s_len):
            acc = dw_ref[0:1, cs] * ext(s, cs)
            for k in range(1, CM_KERNEL):
                acc = acc + dw_ref[k:k + 1, cs] * ext(s + k, cs)
            cv_scr[s, :, cs] = acc + db_ref[:, cs]
        for j in range(W1):
            so_ref[j, :, cs] = ext(j + s_len, cs)

    for s in range(s_len):
        o_ref[s] = _ln_silu(cv_scr[s], lg_ref[...], lb_ref[...])


def cm_mix_sample(ag, st, dw, db_, lg, lb, *, db, s_len, nb):
    C = D_MODEL
    W1 = CM_KERNEL - 1
    row = lambda v: v.reshape(1, C)
    full1 = lambda shape: pl.BlockSpec(shape, lambda i: (0,) * len(shape))
    return pl.pallas_call(
        functools.partial(_cm_mix_sample_kernel, nb=nb, s_len=s_len),
        grid=(db // nb,),
        in_specs=[
            pl.BlockSpec((s_len, nb, 2 * C), lambda i: (0, i, 0)),
            pl.BlockSpec((W1, nb, C), lambda i: (0, i, 0)),
            full1((CM_KERNEL, C)), full1((1, C)), full1((1, C)), full1((1, C)),
        ],
        out_specs=[
            pl.BlockSpec((s_len, nb, C), lambda i: (0, i, 0)),
            pl.BlockSpec((W1, nb, C), lambda i: (0, i, 0)),
        ],
        out_shape=[
            jax.ShapeDtypeStruct((s_len, db, C), F32),
            jax.ShapeDtypeStruct((W1, db, C), F32),
        ],
        scratch_shapes=[pltpu.VMEM((s_len, nb, C), F32), pltpu.VMEM((s_len, nb, C), F32)],
        compiler_params=_cparams(("parallel",), 48),
        name="cm_mix_sample",
    )(ag, st, dw, row(db_), row(lg), row(lb))


def _attn_prompt_kernel(q_ref, k_ref, v_ref, kp_ref, vp_ref, o_ref, *, ch):
    n = pl.program_id(2)
    row = lax.broadcasted_iota(jnp.int32, (ATT_BLOCK, ATT_BLOCK), 0)
    col = lax.broadcasted_iota(jnp.int32, (ATT_BLOCK, ATT_BLOCK), 1)
    lower = col <= row
    upper = col >= row
    upper_first = jnp.logical_and(upper, n > 0)
    lane = lax.broadcasted_iota(jnp.int32, (ATT_BLOCK, LANES), 1)
    neg = -jnp.inf
    dn = (((1,), (1,)), ((), ()))
    for i in range(ch // ATT_BLOCK):
        rows = slice(i * ATT_BLOCK, (i + 1) * ATT_BLOCK)
        prow = slice((i - 1) * ATT_BLOCK, i * ATT_BLOCK)
        lse_tile = jnp.zeros((ATT_BLOCK, LANES), F32)
        for h in range(ATT_HEADS):
            hs = slice(h * HEAD_DIM, (h + 1) * HEAD_DIM)
            q = q_ref[rows, hs]
            kc = k_ref[rows, hs]
            vc = v_ref[rows, hs]
            if i == 0:
                kp = kp_ref[:, hs]
                vp = vp_ref[:, hs]
                pmask = upper_first
            else:
                kp = k_ref[prow, hs]
                vp = v_ref[prow, hs]
                pmask = upper
            sc = lax.dot_general(q, kc, dn, preferred_element_type=F32) * SCALE
            sp = lax.dot_general(q, kp, dn, preferred_element_type=F32) * SCALE
            sc = jnp.where(lower, sc, neg)
            sp = jnp.where(pmask, sp, neg)
            m = jnp.maximum(jnp.max(sc, axis=-1, keepdims=True), jnp.max(sp, axis=-1, keepdims=True))
            pc = jnp.exp(sc - m)
            pp = jnp.exp(sp - m)
            l = jnp.sum(pc, axis=-1, keepdims=True) + jnp.sum(pp, axis=-1, keepdims=True)
            o = jnp.dot(pc.astype(BF16), vc, preferred_element_type=F32)
            o = o + jnp.dot(pp.astype(BF16), vp, preferred_element_type=F32)
            o_ref[rows, hs] = o * (1.0 / l)
            lse_tile = jnp.where(lane == h, m + jnp.log(l), lse_tile)
        o_ref[rows, HD_ALL:O_COLS] = lse_tile


def attn_prompt_group(qkv, g, *, batch, seq):
    win, dil = ATT_GROUPS[g]
    m = seq // dil
    ch = min(m, 512)
    nch = m // ch
    bpc = ch // ATT_BLOCK
    nblk = m // ATT_BLOCK
    view = qkv.reshape(batch * m, dil * QKV_COLS)
    ncb = QKV_COLS // HD_ALL
    cq, ck, cv = 3 * g, 3 * g + 1, 3 * g + 2

    def cur(cb):
        return pl.BlockSpec((ch, HD_ALL), lambda b, r, n: (b * nch + n, r * ncb + cb))

    def prev(cb):
        return pl.BlockSpec((ATT_BLOCK, HD_ALL),
                            lambda b, r, n: (b * nblk + jnp.maximum(n * bpc - 1, 0), r * ncb + cb))

    out = pl.pallas_call(
        functools.partial(_attn_prompt_kernel, ch=ch),
        grid=(batch, dil, nch),
        in_specs=[cur(cq), cur(ck), cur(cv), prev(ck), prev(cv)],
        out_specs=pl.BlockSpec((ch, O_COLS), lambda b, r, n: (b * nch + n, r)),
        out_shape=jax.ShapeDtypeStruct((batch * m, dil * O_COLS), F32),
        compiler_params=_cparams(("parallel", "parallel", "arbitrary"), 48),
        name=f"attn_prompt_g{g}",
    )(view, view, view, view, view)
    return out.reshape(batch * seq, O_COLS)


def _attn_sample_kernel(qkv_ref, c0_ref, c1_ref, c2_ref, o_ref, kn_scr, vn_scr, *, s_len):
    b = pl.program_id(0)

    @pl.when(b == 0)
    def _():
        kn_scr[...] = jnp.zeros_like(kn_scr)
        vn_scr[...] = jnp.zeros_like(vn_scr)

    head_of_lane = lax.broadcasted_iota(jnp.int32, (ATT_HEADS, HD_ALL), 1) // HEAD_DIM
    head_row = lax.broadcasted_iota(jnp.int32, (ATT_HEADS, HD_ALL), 0)
    own = head_of_lane == head_row
    key = lax.broadcasted_iota(jnp.int32, (ATT_HEADS, LANES), 1)
    dn = (((1,), (1,)), ((), ()))
    neg = -jnp.inf
    caches = (c0_ref, c1_ref, c2_ref)

    outs = [[None] * N_GROUPS for _ in range(s_len)]
    lses = [[None] * N_GROUPS for _ in range(s_len)]
    for g in range(N_GROUPS):
        base = g * GROUP_COLS
        kn_scr[g, 0:s_len, :] = qkv_ref[0, :, base + HD_ALL:base + 2 * HD_ALL]
        vn_scr[g, 0:s_len, :] = qkv_ref[0, :, base + 2 * HD_ALL:base + 3 * HD_ALL]
        kn = kn_scr[g].astype(BF16)
        vn = vn_scr[g].astype(BF16)
        for s in range(s_len):
            q = qkv_ref[0, s:s + 1, base:base + HD_ALL]
            wq = jnp.where(own, jnp.broadcast_to(q, (ATT_HEADS, HD_ALL)), 0.0).astype(BF16)
            if g == 0:
                kb = caches[0][0, :, 0:HD_ALL]
                vb = caches[0][0, :, HD_ALL:2 * HD_ALL]
                bmask = key >= s
                nmask = key <= s
            else:
                off = s * 2 * HD_ALL
                kb = caches[g][0, :, off:off + HD_ALL]
                vb = caches[g][0, :, off + HD_ALL:off + 2 * HD_ALL]
                bmask = None
                nmask = key == s
            sb = lax.dot_general(wq, kb.astype(BF16), dn, preferred_element_type=F32) * SCALE
            sn = lax.dot_general(wq, kn, dn, preferred_element_type=F32) * SCALE
            if bmask is not None:
                sb = jnp.where(bmask, sb, neg)
            sn = jnp.where(nmask, sn, neg)
            m = jnp.maximum(jnp.max(sb, axis=-1, keepdims=True), jnp.max(sn, axis=-1, keepdims=True))
            pb = jnp.exp(sb - m)
            pn = jnp.exp(sn - m)
            l = jnp.sum(pb, axis=-1, keepdims=True) + jnp.sum(pn, axis=-1, keepdims=True)
            o = jnp.dot(pb.astype(BF16), vb.astype(BF16), preferred_element_type=F32)
            o = o + jnp.dot(pn.astype(BF16), vn, preferred_element_type=F32)
            outs[s][g] = o * (1.0 / l)
            lses[s][g] = m + jnp.log(l)

    for s in range(s_len):
        mx = jnp.maximum(jnp.maximum(lses[s][0], lses[s][1]), lses[s][2])
        es = [jnp.exp(l - mx) for l in lses[s]]
        inv = 1.0 / (es[0] + es[1] + es[2])
        acc = (es[0] * inv) * outs[s][0] + (es[1] * inv) * outs[s][1] + (es[2] * inv) * outs[s][2]
        o_ref[0, s:s + 1, :] = jnp.sum(jnp.where(own, acc, 0.0), axis=0, keepdims=True)


def attn_sample(qkv, c0, c1, c2, *, db, s_len):
    kv = 2 * HD_ALL
    return pl.pallas_call(
        functools.partial(_attn_sample_kernel, s_len=s_len),
        grid=(db,),
        in_specs=[
            pl.BlockSpec((1, s_len, QKV_COLS), lambda b: (b, 0, 0)),
            pl.BlockSpec((1, ATT_BLOCK, kv), lambda b: (b, 0, 0)),
            pl.BlockSpec((1, ATT_BLOCK, s_len * kv), lambda b: (b, 0, 0)),
            pl.BlockSpec((1, ATT_BLOCK, s_len * kv), lambda b: (b, 0, 0)),
        ],
        out_specs=pl.BlockSpec((1, s_len, HD_ALL), lambda b: (b, 0, 0)),
        out_shape=jax.ShapeDtypeStruct((db, s_len, HD_ALL), F32),
        scratch_shapes=[pltpu.VMEM((N_GROUPS, ATT_BLOCK, HD_ALL), F32),
                        pltpu.VMEM((N_GROUPS, ATT_BLOCK, HD_ALL), F32)],
        compiler_params=_cparams(("arbitrary",), 56),
        name="attn_sample",
    )(qkv, c0, c1, c2)


def _gate_band(ga_w, gx_w):
    eye = jnp.eye(LRU_BLOCKS, dtype=F32)

    def dense(w):
        d = jnp.einsum('ncd,nm->ncmd', w, eye).reshape(D_RNN, D_RNN)
        return jnp.pad(d, ((LANES, LANES), (0, 0)))

    da, dx = dense(ga_w), dense(gx_w)
    bands = [jnp.concatenate([da[c * LANES:(c + 3) * LANES, c * LANES:(c + 1) * LANES],
                              dx[c * LANES:(c + 3) * LANES, c * LANES:(c + 1) * LANES]], axis=1)
             for c in range(N_CT)]
    return jnp.stack(bands).astype(BF16)


def _rope_tables(pos):
    half = HEAD_DIM // 2
    inv_freq = ROPE_THETA ** (-jnp.arange(half, dtype=F32) / half)
    ang = pos.astype(F32)[:, None] * inv_freq
    cos, sin = jnp.cos(ang), jnp.sin(ang)
    return jnp.concatenate([cos, cos], axis=-1), jnp.concatenate([-sin, sin], axis=-1)


def _row_tile(n, cap):
    t = cap
    while n % t:
        t //= 2
    return t


def kernel(x_prompt, x_sample, state_lru_conv, state_lru_h, state_cm_conv, cache_kv_w128, cache_kv_w512, cache_kv_w2048, norm_mix, norm_ffn, norm_final, lru_w_in, lru_conv_w, lru_conv_b, lru_gate_a_w, lru_gate_a_b, lru_gate_x_w, lru_gate_x_b, lru_lambda, lru_w_out, cm_w_pw1, cm_b_pw1, cm_dw_w, cm_dw_b, cm_ln_g, cm_ln_b, cm_w_pw2, cm_b_pw2, att_w_qkv, att_w_o, ffn_w1, ffn_w2):
    B, T, D = x_prompt.shape
    DB, S, _ = x_sample.shape
    depth = norm_mix.shape[0]
    NP, NS = B * T, DB * S
    assert D == D_MODEL and T % (ATT_GROUPS[-1][1] * ATT_BLOCK) == 0 and S == 4
    for c, (win, _) in zip((cache_kv_w128, cache_kv_w512, cache_kv_w2048), ATT_GROUPS):
        assert c.shape[2] == win
    tmp = _row_tile(NP, 1024)
    tms = _row_tile(NS, 512)
    zeros_d = jnp.zeros((D,), F32)

    xp = x_prompt.reshape(NP, D)
    xs = jnp.swapaxes(x_sample, 0, 1).reshape(NS, D)

    cos_p, sin_p = _rope_tables(jnp.arange(T))
    cos_s, sin_s = _rope_tables(PAST_LEN + jnp.arange(NS) // DB)

    p_lru_conv, p_lru_h, s_lru_conv, s_lru_h = [], [], [], []
    p_cm_conv, s_cm_conv = [], []
    p_kv = [[] for _ in ATT_GROUPS]
    s_kv = [[] for _ in ATT_GROUPS]

    for i in range(depth):
        m, j = i % 3, i // 3
        if m == 0:
            w_in = lru_w_in[j].astype(BF16)
            w_out = lru_w_out[j].astype(BF16)
            wb = _gate_band(lru_gate_a_w[j], lru_gate_x_w[j])
            zb = jnp.zeros((2 * D_RNN,), F32)
            lp = (lru_conv_w[j], lru_conv_b[j], wb, lru_gate_a_b[j], lru_gate_x_b[j], lru_lambda[j])
            gu = norm_proj(xp, norm_mix[i], w_in, zb, tm=tmp, tn=D_RNN)
            hg, cp, hp = lru_mix_prompt(gu, *lp, batch=B, seq=T, tt=256)
            xp = out_proj(hg, w_out, zeros_d, xp, tm=tmp)
            p_lru_conv.append(cp)
            p_lru_h.append(hp.reshape(B, D_RNN))
            gus = norm_proj(xs, norm_mix[i], w_in, zb, tm=tms, tn=D_RNN)
            hgs, cs, hs = lru_mix_sample(gus.reshape(S, DB, 2 * D_RNN), jnp.swapaxes(state_lru_conv[j], 0, 1),
                                         state_lru_h[j], *lp, db=DB, s_len=S)
            xs = out_proj(hgs.reshape(NS, D_RNN), w_out, zeros_d, xs, tm=tms)
            s_lru_conv.append(jnp.swapaxes(cs, 0, 1))
            s_lru_h.append(hs)
        elif m == 1:
            w1 = cm_w_pw1[j].astype(BF16)
            w2 = cm_w_pw2[j].astype(BF16)
            cp_ = (cm_dw_w[j], cm_dw_b[j], cm_ln_g[j], cm_ln_b[j])
            ag = norm_proj(xp, norm_mix[i], w1, cm_b_pw1[j], tm=tmp, tn=1024)
            y, cp = cm_mix_prompt(ag, *cp_, batch=B, seq=T, tt=256)
            xp = out_proj(y, w2, cm_b_pw2[j], xp, tm=tmp)
            p_cm_conv.append(cp)
            ags = norm_proj(xs, norm_mix[i], w1, cm_b_pw1[j], tm=tms, tn=1024)
            nb = _row_tile(DB, 32)
            ys, cs = cm_mix_sample(ags.reshape(S, DB, 2 * D), jnp.swapaxes(state_cm_conv[j], 0, 1), *cp_,
                                   db=DB, s_len=S, nb=nb)
            xs = out_proj(ys.reshape(NS, D), w2, cm_b_pw2[j], xs, tm=tms)
            s_cm_conv.append(jnp.swapaxes(cs, 0, 1))
        else:
            wqkv = att_w_qkv[j].astype(BF16)
            wo = att_w_o[j].astype(BF16)
            qkv = norm_qkv(xp, norm_mix[i], wqkv, cos_p, sin_p, tm=tmp, out_dtype=BF16)
            og = [attn_prompt_group(qkv, g, batch=B, seq=T) for g in range(N_GROUPS)]
            xp = merge_out_proj(og[0], og[1], og[2], wo, xp, tm=_row_tile(NP, 512))
            qkv3 = qkv.reshape(B, T, QKV_COLS)
            for g, (win, _) in enumerate(ATT_GROUPS):
                keep = min(win, T)
                kvg = qkv3[:, T - keep:, g * GROUP_COLS + HD_ALL:(g + 1) * GROUP_COLS]
                p_kv[g].append(kvg.astype(F32).reshape(B, keep, 2, ATT_HEADS, HEAD_DIM))
            qkvs = norm_qkv(xs, norm_mix[i], wqkv, cos_s, sin_s, tm=tms, out_dtype=F32)
            qkvs3 = jnp.swapaxes(qkvs.reshape(S, DB, QKV_COLS), 0, 1)
            kvw = 2 * HD_ALL
            caches = [c[j].reshape(DB, ATT_BLOCK, (c.shape[2] // ATT_BLOCK) * kvw)
                      for c in (cache_kv_w128, cache_kv_w512, cache_kv_w2048)]
            os_ = attn_sample(qkvs3, *caches, db=DB, s_len=S)
            xs = out_proj(jnp.swapaxes(os_, 0, 1).reshape(NS, HD_ALL), wo, zeros_d, xs, tm=tms)
            for g in range(N_GROUPS):
                kvg = qkvs3[:, :, g * GROUP_COLS + HD_ALL:(g + 1) * GROUP_COLS]
                s_kv[g].append(kvg.reshape(DB, S, 2, ATT_HEADS, HEAD_DIM))
        final = i == depth - 1
        w1 = ffn_w1[i].astype(BF16)
        w2 = ffn_w2[i].astype(BF16)
        xp = ffn(xp, norm_ffn[i], w1, w2, norm_final, tm=tmp, tf=512, final=final)
        xs = ffn(xs, norm_ffn[i], w1, w2, norm_final, tm=tms, tf=512, final=final)

    return (xp.reshape(B, T, D), jnp.swapaxes(xs.reshape(S, DB, D), 0, 1),
            jnp.stack(p_lru_conv), jnp.stack(p_lru_h), jnp.stack(p_cm_conv),
            jnp.stack(p_kv[0]), jnp.stack(p_kv[1]), jnp.stack(p_kv[2]),
            jnp.stack(s_lru_conv), jnp.stack(s_lru_h), jnp.stack(s_cm_conv),
            jnp.stack(s_kv[0]), jnp.stack(s_kv[1]), jnp.stack(s_kv[2]))
```

```python
import functools

import jax
import jax.numpy as jnp
from jax import lax
from jax.experimental import pallas as pl
from jax.experimental.pallas import tpu as pltpu

F32 = jnp.float32
BF16 = jnp.bfloat16

LANES = 128
D_MODEL = 1024
D_RNN = 1408
LRU_BLOCKS = 16
LRU_BLOCK_DIM = D_RNN // LRU_BLOCKS
LRU_CONV_W = 4
LRU_C = 8.0
CM_KERNEL = 31
ATT_GROUPS = ((128, 1), (512, 4), (2048, 16))
N_GROUPS = 3
ATT_HEADS = 8
HEAD_DIM = 128
ATT_BLOCK = 128
ROPE_THETA = 10000.0
PAST_LEN = 2048
EPS = 1e-6
QKV_COLS = N_GROUPS * 3 * ATT_HEADS * HEAD_DIM
GROUP_COLS = 3 * ATT_HEADS * HEAD_DIM
HD_ALL = ATT_HEADS * HEAD_DIM
O_COLS = HD_ALL + LANES
N_CT = D_RNN // LANES
SCALE = HEAD_DIM ** -0.5
SCALE_LOG2E = SCALE * 1.4426950408889634
MIB = 1024 * 1024


def _cparams(sem, vmem_mib):
    return pltpu.CompilerParams(dimension_semantics=sem, vmem_limit_bytes=vmem_mib * MIB)


def _rms(x, g):
    ms = jnp.mean(x * x, axis=-1, keepdims=True)
    return (x * lax.rsqrt(ms + EPS)) * g


def _sigmoid(x):
    return 1.0 / (1.0 + jnp.exp(-x))


def _softplus(x):
    return jnp.maximum(x, 0.0) + jnp.log1p(jnp.exp(-jnp.abs(x)))


def _gelu_tanh(x):
    return x * (0.5 * (1.0 + jnp.tanh(0.7978845608028654 * (x + 0.044715 * (x * x * x)))))


def _norm_proj_kernel(x_ref, g_ref, w_ref, o_ref, xn_ref):
    @pl.when(pl.program_id(1) == 0)
    def _():
        xn_ref[...] = _rms(x_ref[...], g_ref[...]).astype(BF16)

    o_ref[...] = jnp.dot(xn_ref[...], w_ref[...], preferred_element_type=F32)


def norm_proj(x, g, w, *, tm, tn):
    n, d = x.shape
    nout = w.shape[1]
    return pl.pallas_call(
        _norm_proj_kernel,
        grid=(n // tm, nout // tn),
        in_specs=[
            pl.BlockSpec((tm, d), lambda i, j: (i, 0)),
            pl.BlockSpec((1, d), lambda i, j: (0, 0)),
            pl.BlockSpec((d, tn), lambda i, j: (0, j)),
        ],
        out_specs=pl.BlockSpec((tm, tn), lambda i, j: (i, j)),
        out_shape=jax.ShapeDtypeStruct((n, nout), F32),
        scratch_shapes=[pltpu.VMEM((tm, d), BF16)],
        compiler_params=_cparams(("parallel", "arbitrary"), 48),
        name="norm_proj",
    )(x, g.reshape(1, d), w)


def _norm_glu_kernel(x_ref, g_ref, wa_ref, wg_ref, ba_ref, bg_ref, o_ref, xn_ref):
    @pl.when(pl.program_id(1) == 0)
    def _():
        xn_ref[...] = _rms(x_ref[...], g_ref[...]).astype(BF16)

    xn = xn_ref[...]
    a = jnp.dot(xn, wa_ref[...], preferred_element_type=F32) + ba_ref[...]
    gate = jnp.dot(xn, wg_ref[...], preferred_element_type=F32) + bg_ref[...]
    o_ref[...] = a * _sigmoid(gate)


def norm_glu(x, g, w, b, *, tm, tn):
    n, d = x.shape
    half = w.shape[1] // 2
    nb = half // tn
    b2 = b.reshape(1, 2 * half)
    return pl.pallas_call(
        _norm_glu_kernel,
        grid=(n // tm, nb),
        in_specs=[
            pl.BlockSpec((tm, d), lambda i, j: (i, 0)),
            pl.BlockSpec((1, d), lambda i, j: (0, 0)),
            pl.BlockSpec((d, tn), lambda i, j: (0, j)),
            pl.BlockSpec((d, tn), lambda i, j: (0, j + nb)),
            pl.BlockSpec((1, tn), lambda i, j: (0, j)),
            pl.BlockSpec((1, tn), lambda i, j: (0, j + nb)),
        ],
        out_specs=pl.BlockSpec((tm, tn), lambda i, j: (i, j)),
        out_shape=jax.ShapeDtypeStruct((n, half), F32),
        scratch_shapes=[pltpu.VMEM((tm, d), BF16)],
        compiler_params=_cparams(("parallel", "arbitrary"), 48),
        name="norm_glu",
    )(x, g.reshape(1, d), w, w, b2, b2)


def _norm_qkv_kernel(x_ref, g_ref, w_ref, cos_ref, sin_ref, o_ref, xn_ref):
    @pl.when(pl.program_id(1) == 0)
    def _():
        xn_ref[...] = _rms(x_ref[...], g_ref[...]).astype(BF16)

    xn = xn_ref[...]
    cos = cos_ref[...]
    sin = sin_ref[...]
    for part in range(3):
        acc = jnp.dot(xn, w_ref[:, part * HD_ALL:(part + 1) * HD_ALL], preferred_element_type=F32)
        for h in range(ATT_HEADS):
            hs = slice(h * HEAD_DIM, (h + 1) * HEAD_DIM)
            xh = acc[:, hs]
            if part < 2:
                xh = xh * cos + pltpu.roll(xh, HEAD_DIM // 2, 1) * sin
            o_ref[:, part * HD_ALL + h * HEAD_DIM:part * HD_ALL + (h + 1) * HEAD_DIM] = xh.astype(o_ref.dtype)


def norm_qkv(x, g, w, cos, sin, *, tm, out_dtype):
    n, d = x.shape
    period = cos.shape[0]
    nper = period // tm
    return pl.pallas_call(
        _norm_qkv_kernel,
        grid=(n // tm, N_GROUPS),
        in_specs=[
            pl.BlockSpec((tm, d), lambda i, j: (i, 0)),
            pl.BlockSpec((1, d), lambda i, j: (0, 0)),
            pl.BlockSpec((d, GROUP_COLS), lambda i, j: (0, j)),
            pl.BlockSpec((tm, HEAD_DIM), lambda i, j: (i % nper, 0)),
            pl.BlockSpec((tm, HEAD_DIM), lambda i, j: (i % nper, 0)),
        ],
        out_specs=pl.BlockSpec((tm, GROUP_COLS), lambda i, j: (i, j)),
        out_shape=jax.ShapeDtypeStruct((n, QKV_COLS), out_dtype),
        scratch_shapes=[pltpu.VMEM((tm, d), BF16)],
        compiler_params=_cparams(("parallel", "arbitrary"), 48),
        name="norm_qkv",
    )(x, g.reshape(1, d), w, cos, sin)


def _ffn_kernel(x_ref, g_ref, w1_ref, w2_ref, gf_ref, o_ref, xn_ref, acc_ref, *, final):
    f = pl.program_id(1)

    @pl.when(f == 0)
    def _():
        xn_ref[...] = _rms(x_ref[...], g_ref[...]).astype(BF16)
        acc_ref[...] = jnp.zeros_like(acc_ref)

    h = jnp.dot(xn_ref[...], w1_ref[...], preferred_element_type=F32)
    h = jnp.maximum(h, 0.0)
    h = (h * h).astype(BF16)
    acc_ref[...] += jnp.dot(h, w2_ref[...], preferred_element_type=F32)

    @pl.when(f == pl.num_programs(1) - 1)
    def _():
        y = x_ref[...] + acc_ref[...]
        if final:
            y = _rms(y, gf_ref[...])
        o_ref[...] = y


def ffn(x, g, w1, w2, gf, *, tm, tf, final):
    n, d = x.shape
    dff = w1.shape[1]
    return pl.pallas_call(
        functools.partial(_ffn_kernel, final=final),
        grid=(n // tm, dff // tf),
        in_specs=[
            pl.BlockSpec((tm, d), lambda i, f: (i, 0)),
            pl.BlockSpec((1, d), lambda i, f: (0, 0)),
            pl.BlockSpec((d, tf), lambda i, f: (0, f)),
            pl.BlockSpec((tf, d), lambda i, f: (f, 0)),
            pl.BlockSpec((1, d), lambda i, f: (0, 0)),
        ],
        out_specs=pl.BlockSpec((tm, d), lambda i, f: (i, 0)),
        out_shape=jax.ShapeDtypeStruct((n, d), F32),
        scratch_shapes=[pltpu.VMEM((tm, d), BF16), pltpu.VMEM((tm, d), F32)],
        compiler_params=_cparams(("parallel", "arbitrary"), 48),
        name="ffn",
    )(x, g.reshape(1, d), w1, w2, gf.reshape(1, d))


def _out_proj_kernel(a_ref, w_ref, b_ref, r_ref, o_ref):
    y = jnp.dot(a_ref[...].astype(BF16), w_ref[...], preferred_element_type=F32)
    o_ref[...] = r_ref[...] + (y + b_ref[...])


def out_proj(a, w, b, resid, *, tm):
    n, k = a.shape
    d = w.shape[1]
    return pl.pallas_call(
        _out_proj_kernel,
        grid=(n // tm,),
        in_specs=[
            pl.BlockSpec((tm, k), lambda i: (i, 0)),
            pl.BlockSpec((k, d), lambda i: (0, 0)),
            pl.BlockSpec((1, d), lambda i: (0, 0)),
            pl.BlockSpec((tm, d), lambda i: (i, 0)),
        ],
        out_specs=pl.BlockSpec((tm, d), lambda i: (i, 0)),
        out_shape=jax.ShapeDtypeStruct((n, d), F32),
        compiler_params=_cparams(("parallel",), 48),
        name="out_proj",
    )(a, w, b.reshape(1, d), resid)


def _merge_out_proj_kernel(o0_ref, o1_ref, o2_ref, w_ref, r_ref, o_ref, m_scr):
    o_refs = (o0_ref, o1_ref, o2_ref)
    lses = [r[:, HD_ALL:HD_ALL + ATT_HEADS] for r in o_refs]
    mx = jnp.maximum(jnp.maximum(lses[0], lses[1]), lses[2])
    es = [jnp.exp(l - mx) for l in lses]
    inv = 1.0 / (es[0] + es[1] + es[2])
    ws = [e * inv for e in es]
    for h in range(ATT_HEADS):
        sl = slice(h * HEAD_DIM, (h + 1) * HEAD_DIM)
        acc = ws[0][:, h:h + 1] * o0_ref[:, sl]
        acc = acc + ws[1][:, h:h + 1] * o1_ref[:, sl]
        acc = acc + ws[2][:, h:h + 1] * o2_ref[:, sl]
        m_scr[:, sl] = acc.astype(BF16)
    o_ref[...] = r_ref[...] + jnp.dot(m_scr[...], w_ref[...], preferred_element_type=F32)


def merge_out_proj(o0, o1, o2, w, resid, *, tm):
    n = o0.shape[0]
    d = w.shape[1]
    ospec = pl.BlockSpec((tm, O_COLS), lambda i: (i, 0))
    return pl.pallas_call(
        _merge_out_proj_kernel,
        grid=(n // tm,),
        in_specs=[ospec, ospec, ospec,
                  pl.BlockSpec((HD_ALL, d), lambda i: (0, 0)),
                  pl.BlockSpec((tm, d), lambda i: (i, 0))],
        out_specs=pl.BlockSpec((tm, d), lambda i: (i, 0)),
        out_shape=jax.ShapeDtypeStruct((n, d), F32),
        scratch_shapes=[pltpu.VMEM((tm, HD_ALL), BF16)],
        compiler_params=_cparams(("parallel",), 48),
        name="merge_out_proj",
    )(o0, o1, o2, w, resid)


def _band_slices(c):
    lo = max(c - 1, 0)
    hi = min(c + 2, N_CT)
    return (lo * LANES, hi * LANES), ((lo - (c - 1)) * LANES, (hi - (c - 1)) * LANES)


def _lru_ab(za, zx, u, sp):
    r = _sigmoid(za)
    i = _sigmoid(zx)
    log_a = (-LRU_C * r) * sp
    a = jnp.exp(log_a)
    b = jnp.sqrt(-jnp.tanh(log_a) * (a * a + 1.0)) * (i * u)
    return a, b


def _lru_gates(xg_tile_fn, wb_ref, c, u, gab, gxb, sp):
    (xlo, xhi), (rlo, rhi) = _band_slices(c)
    z = jnp.dot(xg_tile_fn(xlo, xhi), wb_ref[c, rlo:rhi, :], preferred_element_type=F32)
    return _lru_ab(z[:, :LANES] + gab, z[:, LANES:] + gxb, u, sp)


def _lru_mix_prompt_kernel(gu_ref, cw_ref, cb_ref, wb_ref, gab_ref, gxb_ref, lam_ref,
                           hg_ref, cs_ref, hl_ref, ubuf, cv_scr, xg_scr, hc_scr, *, tt):
    t = pl.program_id(1)
    nt = pl.num_programs(1)
    C = D_RNN
    RB = 64
    SUB = 8

    @pl.when(t == 0)
    def _():
        ubuf[0:8, :] = jnp.zeros((8, C), F32)
        hc_scr[...] = jnp.zeros((SUB, C), F32)

    ubuf[8:8 + tt, :] = gu_ref[:, C:2 * C]

    for c in range(N_CT):
        cs = slice(c * LANES, (c + 1) * LANES)
        for r0 in range(0, tt, RB):
            acc = cw_ref[0:1, cs] * ubuf[r0 + 5:r0 + 5 + RB, cs]
            for k in range(1, LRU_CONV_W):
                acc = acc + cw_ref[k:k + 1, cs] * ubuf[r0 + 5 + k:r0 + 5 + k + RB, cs]
            acc = acc + cb_ref[:, cs]
            cv_scr[r0:r0 + RB, cs] = acc
            xg_scr[r0:r0 + RB, cs] = acc.astype(BF16)

    @pl.when(t == nt - 1)
    def _():
        cs_ref[0] = ubuf[tt + 5:tt + 8, :]

    ubuf[0:8, :] = ubuf[tt:tt + 8, :]

    sub = lax.broadcasted_iota(jnp.int32, (SUB, LANES), 0)
    keep = [sub >= d for d in (1, 2, 4)]
    sp = _softplus(-lam_ref[...])
    for c in range(N_CT):
        cs = slice(c * LANES, (c + 1) * LANES)
        (xlo, xhi), (rlo, rhi) = _band_slices(c)
        z = jnp.dot(xg_scr[:, xlo:xhi], wb_ref[c, rlo:rhi, :], preferred_element_type=F32)
        hin = hc_scr[:, cs]
        for r0 in range(0, tt, 2 * SUB):
            hs = []
            for r1 in (r0, r0 + SUB):
                a, b = _lru_ab(z[r1:r1 + SUB, :LANES] + gab_ref[:, cs], z[r1:r1 + SUB, LANES:] + gxb_ref[:, cs],
                               cv_scr[r1:r1 + SUB, cs], sp[:, cs])
                for d, kp in zip((1, 2, 4), keep):
                    a_sh = jnp.where(kp, pltpu.roll(a, d, 0), 1.0)
                    b_sh = jnp.where(kp, pltpu.roll(b, d, 0), 0.0)
                    b = a * b_sh + b
                    a = a * a_sh
                h = b + a * hin
                hin = jnp.broadcast_to(h[SUB - 1:SUB, :], (SUB, LANES))
                hs.append(h)
            h2 = jnp.concatenate(hs, axis=0)
            hg_ref[r0:r0 + 2 * SUB, cs] = (h2 * _gelu_tanh(gu_ref[r0:r0 + 2 * SUB, cs])).astype(hg_ref.dtype)
        hc_scr[:, cs] = hin

    @pl.when(t == nt - 1)
    def _():
        hl_ref[0] = hc_scr[0:1, :]


def lru_mix_prompt(gu, cw, cb, wb, gab, gxb, lam, *, batch, seq, tt):
    C = D_RNN
    nt = seq // tt
    row = lambda v: v.reshape(1, C)
    full2 = lambda shape: pl.BlockSpec(shape, lambda b, t: (0,) * len(shape))
    return pl.pallas_call(
        functools.partial(_lru_mix_prompt_kernel, tt=tt),
        grid=(batch, nt),
        in_specs=[
            pl.BlockSpec((tt, 2 * C), lambda b, t: (b * nt + t, 0)),
            full2((LRU_CONV_W, C)), full2((1, C)), full2((N_CT, 3 * LANES, 2 * LANES)),
            full2((1, C)), full2((1, C)), full2((1, C)),
        ],
        out_specs=[
            pl.BlockSpec((tt, C), lambda b, t: (b * nt + t, 0)),
            pl.BlockSpec((1, LRU_CONV_W - 1, C), lambda b, t: (b, 0, 0)),
            pl.BlockSpec((1, 1, C), lambda b, t: (b, 0, 0)),
        ],
        out_shape=[
            jax.ShapeDtypeStruct((batch * seq, C), BF16),
            jax.ShapeDtypeStruct((batch, LRU_CONV_W - 1, C), F32),
            jax.ShapeDtypeStruct((batch, 1, C), F32),
        ],
        scratch_shapes=[
            pltpu.VMEM((tt + 8, C), F32), pltpu.VMEM((tt, C), F32), pltpu.VMEM((tt, C), BF16),
            pltpu.VMEM((8, C), F32),
        ],
        compiler_params=_cparams(("arbitrary", "arbitrary"), 48),
        name="lru_mix_prompt",
    )(gu, cw, row(cb), wb, row(gab), row(gxb), row(lam))


def _lru_mix_sample_kernel(gu_ref, cst_ref, hp_ref, cw_ref, cb_ref, wb_ref, gab_ref, gxb_ref, lam_ref,
                           hg_ref, cso_ref, hl_ref, cv_scr, xg_scr, *, db, s_len):
    del db
    C = D_RNN
    W1 = LRU_CONV_W - 1

    def ext(j, cs):
        if j < W1:
            return cst_ref[j, :, cs]
        return gu_ref[j - W1, :, C + cs.start:C + cs.stop]

    for c in range(N_CT):
        cs = slice(c * LANES, (c + 1) * LANES)
        for s in range(s_len):
            acc = cw_ref[0:1, cs] * ext(s, cs)
            for k in range(1, LRU_CONV_W):
                acc = acc + cw_ref[k:k + 1, cs] * ext(s + k, cs)
            acc = acc + cb_ref[:, cs]
            cv_scr[s, :, cs] = acc
            xg_scr[s, :, cs] = acc.astype(BF16)
        for k in range(W1):
            cso_ref[k, :, cs] = ext(s_len + k, cs)

    sp = _softplus(-lam_ref[...])
    for c in range(N_CT):
        cs = slice(c * LANES, (c + 1) * LANES)
        h = hp_ref[:, cs]
        for s in range(s_len):
            a, b = _lru_gates(lambda lo, hi: xg_scr[s, :, lo:hi], wb_ref, c, cv_scr[s, :, cs],
                              gab_ref[:, cs], gxb_ref[:, cs], sp[:, cs])
            h = a * h + b
            hg_ref[s, :, cs] = h * _gelu_tanh(gu_ref[s, :, cs])
        hl_ref[:, cs] = h


def lru_mix_sample(gu, cst, hp, cw, cb, wb, gab, gxb, lam, *, db, s_len):
    C = D_RNN
    row = lambda v: v.reshape(1, C)
    return pl.pallas_call(
        functools.partial(_lru_mix_sample_kernel, db=db, s_len=s_len),
        out_shape=[
            jax.ShapeDtypeStruct((s_len, db, C), F32),
            jax.ShapeDtypeStruct((LRU_CONV_W - 1, db, C), F32),
            jax.ShapeDtypeStruct((db, C), F32),
        ],
        scratch_shapes=[pltpu.VMEM((s_len, db, C), F32), pltpu.VMEM((s_len, db, C), BF16)],
        compiler_params=pltpu.CompilerParams(vmem_limit_bytes=48 * MIB),
        name="lru_mix_sample",
    )(gu, cst, hp, cw, row(cb), wb, row(gab), row(gxb), row(lam))


def _ln_silu(y, g, b):
    mu = jnp.mean(y, axis=-1, keepdims=True)
    yc = y - mu
    var = jnp.mean(yc * yc, axis=-1, keepdims=True)
    z = yc * lax.rsqrt(var + EPS) * g + b
    return z * _sigmoid(z)


def _cm_mix_prompt_kernel(u_ref, dw_ref, db_ref, lg_ref, lb_ref, o_ref, cs_ref, ubuf, cv_scr, *, tt):
    t = pl.program_id(1)
    nt = pl.num_programs(1)
    C = D_MODEL
    RB = 64
    HALO = 32
    OFF = HALO - (CM_KERNEL - 1)

    @pl.when(t == 0)
    def _():
        ubuf[0:HALO, :] = jnp.zeros((HALO, C), F32)

    ubuf[HALO:HALO + tt, :] = u_ref[...]

    WIN = RB + HALO
    for c in range(C // LANES):
        cs = slice(c * LANES, (c + 1) * LANES)
        for r0 in range(0, tt, RB):
            x = ubuf[r0:r0 + WIN, cs]
            acc = None
            for r in range(8):
                xr = x if r == 0 else pltpu.roll(x, WIN - r, 0)
                for o in range(OFF, OFF + CM_KERNEL):
                    if o % 8 != r:
                        continue
                    term = dw_ref[o - OFF:o - OFF + 1, cs] * xr[o - r:o - r + RB, :]
                    acc = term if acc is None else acc + term
            cv_scr[r0:r0 + RB, cs] = acc + db_ref[:, cs]

    @pl.when(t == nt - 1)
    def _():
        cs_ref[0] = ubuf[tt + OFF:tt + HALO, :]

    ubuf[0:HALO, :] = ubuf[tt:tt + HALO, :]

    for r0 in range(0, tt, RB):
        o_ref[r0:r0 + RB, :] = _ln_silu(cv_scr[r0:r0 + RB, :], lg_ref[...], lb_ref[...]).astype(o_ref.dtype)


def cm_mix_prompt(u, dw, db_, lg, lb, *, batch, seq, tt):
    C = D_MODEL
    nt = seq // tt
    row = lambda v: v.reshape(1, C)
    full2 = lambda shape: pl.BlockSpec(shape, lambda b, t: (0,) * len(shape))
    return pl.pallas_call(
        functools.partial(_cm_mix_prompt_kernel, tt=tt),
        grid=(batch, nt),
        in_specs=[
            pl.BlockSpec((tt, C), lambda b, t: (b * nt + t, 0)),
            full2((CM_KERNEL, C)), full2((1, C)), full2((1, C)), full2((1, C)),
        ],
        out_specs=[
            pl.BlockSpec((tt, C), lambda b, t: (b * nt + t, 0)),
            pl.BlockSpec((1, CM_KERNEL - 1, C), lambda b, t: (b, 0, 0)),
        ],
        out_shape=[
            jax.ShapeDtypeStruct((batch * seq, C), BF16),
            jax.ShapeDtypeStruct((batch, CM_KERNEL - 1, C), F32),
        ],
        scratch_shapes=[pltpu.VMEM((tt + 32, C), F32), pltpu.VMEM((tt, C), F32)],
        compiler_params=_cparams(("arbitrary", "arbitrary"), 48),
        name="cm_mix_prompt",
    )(u, dw, row(db_), row(lg), row(lb))


def _cm_mix_sample_kernel(u_ref, st_ref, dw_ref, db_ref, lg_ref, lb_ref, o_ref, so_ref, cv_scr, *, nb, s_len):
    del nb
    C = D_MODEL
    W1 = CM_KERNEL - 1

    def ext(j, cs):
        if j < W1:
            return st_ref[j, :, cs]
        return u_ref[j - W1, :, cs]

    for c in range(C // LANES):
        cs = slice(c * LANES, (c + 1) * LANES)
        for s in range(s_len):
            acc = dw_ref[0:1, cs] * ext(s, cs)
            for k in range(1, CM_KERNEL):
                acc = acc + dw_ref[k:k + 1, cs] * ext(s + k, cs)
            cv_scr[s, :, cs] = acc + db_ref[:, cs]
        for j in range(W1):
            so_ref[j, :, cs] = ext(j + s_len, cs)

    for s in range(s_len):
        o_ref[s] = _ln_silu(cv_scr[s], lg_ref[...], lb_ref[...])


def cm_mix_sample(u, st, dw, db_, lg, lb, *, db, s_len, nb):
    C = D_MODEL
    W1 = CM_KERNEL - 1
    row = lambda v: v.reshape(1, C)
    full1 = lambda shape: pl.BlockSpec(shape, lambda i: (0,) * len(shape))
    return pl.pallas_call(
        functools.partial(_cm_mix_sample_kernel, nb=nb, s_len=s_len),
        grid=(db // nb,),
        in_specs=[
            pl.BlockSpec((s_len, nb, C), lambda i: (0, i, 0)),
            pl.BlockSpec((W1, nb, C), lambda i: (0, i, 0)),
            full1((CM_KERNEL, C)), full1((1, C)), full1((1, C)), full1((1, C)),
        ],
        out_specs=[
            pl.BlockSpec((s_len, nb, C), lambda i: (0, i, 0)),
            pl.BlockSpec((W1, nb, C), lambda i: (0, i, 0)),
        ],
        out_shape=[
            jax.ShapeDtypeStruct((s_len, db, C), F32),
            jax.ShapeDtypeStruct((W1, db, C), F32),
        ],
        scratch_shapes=[pltpu.VMEM((s_len, nb, C), F32)],
        compiler_params=_cparams(("parallel",), 48),
        name="cm_mix_sample",
    )(u, st, dw, row(db_), row(lg), row(lb))


def _attn_prompt_kernel(q_ref, k_ref, v_ref, kp_ref, vp_ref, o_ref, kext, vext, s_scr, *, ch):
    n = pl.program_id(2)
    kext[0:ATT_BLOCK, :] = kp_ref[...]
    kext[ATT_BLOCK:ATT_BLOCK + ch, :] = k_ref[...]
    vext[0:ATT_BLOCK, :] = vp_ref[...]
    vext[ATT_BLOCK:ATT_BLOCK + ch, :] = v_ref[...]
    qi = lax.broadcasted_iota(jnp.int32, (ATT_BLOCK, 2 * ATT_BLOCK), 0)
    kj = lax.broadcasted_iota(jnp.int32, (ATT_BLOCK, 2 * ATT_BLOCK), 1)
    band = jnp.logical_and(kj >= qi, kj <= qi + ATT_BLOCK)
    band_first = jnp.logical_and(band, jnp.logical_or(kj >= ATT_BLOCK, n > 0))
    lane = lax.broadcasted_iota(jnp.int32, (ATT_BLOCK, LANES), 1)
    dn = (((1,), (1,)), ((), ()))
    for i in range(ch // ATT_BLOCK):
        rows = slice(i * ATT_BLOCK, (i + 1) * ATT_BLOCK)
        krows = slice(i * ATT_BLOCK, (i + 2) * ATT_BLOCK)
        mask = band_first if i == 0 else band
        for h in range(ATT_HEADS):
            hs = slice(h * HEAD_DIM, (h + 1) * HEAD_DIM)
            s_scr[i % 2, h] = lax.dot_general(q_ref[rows, hs], kext[krows, hs], dn, preferred_element_type=F32)
        lse_tile = jnp.zeros((ATT_BLOCK, LANES), F32)
        for h in range(ATT_HEADS):
            hs = slice(h * HEAD_DIM, (h + 1) * HEAD_DIM)
            s = jnp.where(mask, s_scr[i % 2, h], -jnp.inf)
            m = jnp.max(jnp.maximum(s[:, :ATT_BLOCK], s[:, ATT_BLOCK:]), axis=-1, keepdims=True)
            p = jnp.exp2((s - m) * SCALE_LOG2E)
            l = jnp.sum(p[:, :ATT_BLOCK] + p[:, ATT_BLOCK:], axis=-1, keepdims=True)
            o = jnp.dot(p.astype(BF16), vext[krows, hs], preferred_element_type=F32)
            o_ref[rows, hs] = o * (1.0 / l)
            lse_tile = jnp.where(lane == h, m * SCALE + jnp.log(l), lse_tile)
        o_ref[rows, HD_ALL:O_COLS] = lse_tile


def attn_prompt_group(qkv, g, *, batch, seq):
    win, dil = ATT_GROUPS[g]
    m = seq // dil
    ch = min(m, 512)
    nch = m // ch
    bpc = ch // ATT_BLOCK
    nblk = m // ATT_BLOCK
    view = qkv.reshape(batch * m, dil * QKV_COLS)
    ncb = QKV_COLS // HD_ALL
    cq, ck, cv = 3 * g, 3 * g + 1, 3 * g + 2

    def cur(cb):
        return pl.BlockSpec((ch, HD_ALL), lambda b, r, n: (b * nch + n, r * ncb + cb))

    def prev(cb):
        return pl.BlockSpec((ATT_BLOCK, HD_ALL),
                            lambda b, r, n: (b * nblk + jnp.maximum(n * bpc - 1, 0), r * ncb + cb))

    out = pl.pallas_call(
        functools.partial(_attn_prompt_kernel, ch=ch),
        grid=(batch, dil, nch),
        in_specs=[cur(cq), cur(ck), cur(cv), prev(ck), prev(cv)],
        out_specs=pl.BlockSpec((ch, O_COLS), lambda b, r, n: (b * nch + n, r)),
        out_shape=jax.ShapeDtypeStruct((batch * m, dil * O_COLS), F32),
        scratch_shapes=[pltpu.VMEM((ATT_BLOCK + ch, HD_ALL), BF16), pltpu.VMEM((ATT_BLOCK + ch, HD_ALL), BF16),
                        pltpu.VMEM((2, ATT_HEADS, ATT_BLOCK, 2 * ATT_BLOCK), F32)],
        compiler_params=_cparams(("parallel", "parallel", "arbitrary"), 48),
        name=f"attn_prompt_g{g}",
    )(view, view, view, view, view)
    return out.reshape(batch * seq, O_COLS)


QKV_TILES = QKV_COLS // LANES
GROUP_TILES = GROUP_COLS // LANES


def _attn_sample_kernel(qkv_ref, c0_ref, c1_ref, c2_ref, o_ref, s_scr, p_scr, *, s_len):
    nk = ATT_BLOCK * ATT_HEADS
    caches = (c0_ref, c1_ref, c2_ref)
    col = lax.broadcasted_iota(jnp.int32, (ATT_HEADS, nk), 1)
    rowh = lax.broadcasted_iota(jnp.int32, (ATT_HEADS, nk), 0)
    own = (col & (ATT_HEADS - 1)) == rowh
    key = col >> 3
    dn = (((1,), (1,)), ((), ()))

    def kvflat(g, s, which):
        r = 0 if g == 0 else s
        return caches[g][0, :, r, which].reshape(nk, HEAD_DIM).astype(BF16)

    def tile(s, g, which):
        lo = g * GROUP_TILES + which * ATT_HEADS
        return qkv_ref[0, s, lo:lo + ATT_HEADS, :]

    def rnd(x):
        return x.astype(BF16).astype(F32)

    q0 = qkv_ref[0, :, 0:ATT_HEADS, :].reshape(s_len * ATT_HEADS, HEAD_DIM).astype(BF16)
    s0 = lax.dot_general(q0, kvflat(0, 0, 0), dn, preferred_element_type=F32)
    s_scr[0:s_len] = s0.reshape(s_len, ATT_HEADS, nk)
    for g in range(1, N_GROUPS):
        for s in range(s_len):
            s_scr[g * s_len + s] = lax.dot_general(tile(s, g, 0).astype(BF16), kvflat(g, s, 0), dn,
                                                   preferred_element_type=F32)

    stats = {}
    for g in range(N_GROUPS):
        for s in range(s_len):
            idx = g * s_len + s
            valid = jnp.logical_and(own, key >= s) if g == 0 else own
            sc = jnp.where(valid, s_scr[idx], -jnp.inf)
            news = tuple(range(s + 1)) if g == 0 else (s,)
            qf = rnd(tile(s, g, 0))
            sn = [jnp.sum(qf * rnd(tile(j, g, 1)), axis=-1, keepdims=True) for j in news]
            m = jnp.max(sc, axis=-1, keepdims=True)
            for x in sn:
                m = jnp.maximum(m, x)
            p = jnp.exp2((sc - m) * SCALE_LOG2E)
            pn = [jnp.exp2((x - m) * SCALE_LOG2E) for x in sn]
            l = jnp.sum(p, axis=-1, keepdims=True)
            for x in pn:
                l = l + x
            p_scr[idx] = p
            stats[(g, s)] = (m, l, pn, news)

    outs = {}
    p0 = p_scr[0:s_len].reshape(s_len * ATT_HEADS, nk).astype(BF16)
    o0 = jnp.dot(p0, kvflat(0, 0, 1), preferred_element_type=F32).reshape(s_len, ATT_HEADS, HEAD_DIM)
    for g in range(N_GROUPS):
        for s in range(s_len):
            idx = g * s_len + s
            m, l, pn, news = stats[(g, s)]
            if g == 0:
                o = o0[s]
            else:
                o = jnp.dot(p_scr[idx].astype(BF16), kvflat(g, s, 1), preferred_element_type=F32)
            for x, j in zip(pn, news):
                o = o + rnd(x) * rnd(tile(j, g, 2))
            outs[(g, s)] = (o * (1.0 / l), m * SCALE + jnp.log(l))

    for s in range(s_len):
        lses = [outs[(g, s)][1] for g in range(N_GROUPS)]
        mx = jnp.maximum(jnp.maximum(lses[0], lses[1]), lses[2])
        es = [jnp.exp(x - mx) for x in lses]
        inv = 1.0 / (es[0] + es[1] + es[2])
        o_ref[0, s] = ((es[0] * inv) * outs[(0, s)][0] + (es[1] * inv) * outs[(1, s)][0]
                       + (es[2] * inv) * outs[(2, s)][0])


def attn_sample(qkv, c0, c1, c2, layer, *, db, s_len):
    nk = ATT_BLOCK * ATT_HEADS

    def cspec(c):
        r = min(c.shape[2], s_len)
        return pl.BlockSpec((1, ATT_BLOCK, r, 2, ATT_HEADS, HEAD_DIM), lambda b: (layer * db + b, 0, 0, 0, 0, 0))

    return pl.pallas_call(
        functools.partial(_attn_sample_kernel, s_len=s_len),
        grid=(db,),
        in_specs=[pl.BlockSpec((1, s_len, QKV_TILES, LANES), lambda b: (b, 0, 0, 0)),
                  cspec(c0), cspec(c1), cspec(c2)],
        out_specs=pl.BlockSpec((1, s_len, ATT_HEADS, HEAD_DIM), lambda b: (b, 0, 0, 0)),
        out_shape=jax.ShapeDtypeStruct((db, s_len, ATT_HEADS, HEAD_DIM), F32),
        scratch_shapes=[pltpu.VMEM((N_GROUPS * s_len, ATT_HEADS, nk), F32),
                        pltpu.VMEM((N_GROUPS * s_len, ATT_HEADS, nk), F32)],
        compiler_params=_cparams(("arbitrary",), 56),
        name="attn_sample",
    )(qkv, c0, c1, c2)


def _gate_band(ga_w, gx_w):
    eye = jnp.eye(LRU_BLOCKS, dtype=F32)

    def dense(w):
        d = jnp.einsum('ncd,nm->ncmd', w, eye).reshape(D_RNN, D_RNN)
        return jnp.pad(d, ((LANES, LANES), (0, 0)))

    da, dx = dense(ga_w), dense(gx_w)
    bands = [jnp.concatenate([da[c * LANES:(c + 3) * LANES, c * LANES:(c + 1) * LANES],
                              dx[c * LANES:(c + 3) * LANES, c * LANES:(c + 1) * LANES]], axis=1)
             for c in range(N_CT)]
    return jnp.stack(bands).astype(BF16)


def _rope_tables(pos):
    half = HEAD_DIM // 2
    inv_freq = ROPE_THETA ** (-jnp.arange(half, dtype=F32) / half)
    ang = pos.astype(F32)[:, None] * inv_freq
    cos, sin = jnp.cos(ang), jnp.sin(ang)
    return jnp.concatenate([cos, cos], axis=-1), jnp.concatenate([-sin, sin], axis=-1)


def _row_tile(n, cap):
    t = cap
    while n % t:
        t //= 2
    return t


def kernel(x_prompt, x_sample, state_lru_conv, state_lru_h, state_cm_conv, cache_kv_w128, cache_kv_w512, cache_kv_w2048, norm_mix, norm_ffn, norm_final, lru_w_in, lru_conv_w, lru_conv_b, lru_gate_a_w, lru_gate_a_b, lru_gate_x_w, lru_gate_x_b, lru_lambda, lru_w_out, cm_w_pw1, cm_b_pw1, cm_dw_w, cm_dw_b, cm_ln_g, cm_ln_b, cm_w_pw2, cm_b_pw2, att_w_qkv, att_w_o, ffn_w1, ffn_w2):
    B, T, D = x_prompt.shape
    DB, S, _ = x_sample.shape
    depth = norm_mix.shape[0]
    NP, NS = B * T, DB * S
    assert D == D_MODEL and T % (ATT_GROUPS[-1][1] * ATT_BLOCK) == 0 and S == 4
    for c, (win, _) in zip((cache_kv_w128, cache_kv_w512, cache_kv_w2048), ATT_GROUPS):
        assert c.shape[2] == win
    tmp = _row_tile(NP, 1024)
    tms = _row_tile(NS, 512)
    zeros_d = jnp.zeros((D,), F32)

    xp = x_prompt.reshape(NP, D)
    xs = jnp.swapaxes(x_sample, 0, 1).reshape(NS, D)

    cos_p, sin_p = _rope_tables(jnp.arange(T))
    cos_s, sin_s = _rope_tables(PAST_LEN + jnp.arange(NS) // DB)

    p_lru_conv, p_lru_h, s_lru_conv, s_lru_h = [], [], [], []
    p_cm_conv, s_cm_conv = [], []
    p_kv = [[] for _ in ATT_GROUPS]
    s_kv = [[] for _ in ATT_GROUPS]

    for i in range(depth):
        m, j = i % 3, i // 3
        if m == 0:
            w_in = lru_w_in[j].astype(BF16)
            w_out = lru_w_out[j].astype(BF16)
            wb = _gate_band(lru_gate_a_w[j], lru_gate_x_w[j])
            lp = (lru_conv_w[j], lru_conv_b[j], wb, lru_gate_a_b[j], lru_gate_x_b[j], lru_lambda[j])
            gu = norm_proj(xp, norm_mix[i], w_in, tm=tmp, tn=D_RNN)
            hg, cp, hp = lru_mix_prompt(gu, *lp, batch=B, seq=T, tt=256)
            xp = out_proj(hg, w_out, zeros_d, xp, tm=tmp)
            p_lru_conv.append(cp)
            p_lru_h.append(hp.reshape(B, D_RNN))
            gus = norm_proj(xs, norm_mix[i], w_in, tm=tms, tn=D_RNN)
            hgs, cs, hs = lru_mix_sample(gus.reshape(S, DB, 2 * D_RNN), jnp.swapaxes(state_lru_conv[j], 0, 1),
                                         state_lru_h[j], *lp, db=DB, s_len=S)
            xs = out_proj(hgs.reshape(NS, D_RNN), w_out, zeros_d, xs, tm=tms)
            s_lru_conv.append(jnp.swapaxes(cs, 0, 1))
            s_lru_h.append(hs)
        elif m == 1:
            w1 = cm_w_pw1[j].astype(BF16)
            w2 = cm_w_pw2[j].astype(BF16)
            cp_ = (cm_dw_w[j], cm_dw_b[j], cm_ln_g[j], cm_ln_b[j])
            u = norm_glu(xp, norm_mix[i], w1, cm_b_pw1[j], tm=tmp, tn=512)
            y, cp = cm_mix_prompt(u, *cp_, batch=B, seq=T, tt=256)
            xp = out_proj(y, w2, cm_b_pw2[j], xp, tm=tmp)
            p_cm_conv.append(cp)
            us = norm_glu(xs, norm_mix[i], w1, cm_b_pw1[j], tm=tms, tn=512)
            nb = _row_tile(DB, 32)
            ys, cs = cm_mix_sample(us.reshape(S, DB, D), jnp.swapaxes(state_cm_conv[j], 0, 1), *cp_,
                                   db=DB, s_len=S, nb=nb)
            xs = out_proj(ys.reshape(NS, D), w2, cm_b_pw2[j], xs, tm=tms)
            s_cm_conv.append(jnp.swapaxes(cs, 0, 1))
        else:
            wqkv = att_w_qkv[j].astype(BF16)
            wo = att_w_o[j].astype(BF16)
            qkv = norm_qkv(xp, norm_mix[i], wqkv, cos_p, sin_p, tm=_row_tile(NP, 512), out_dtype=BF16)
            og = [attn_prompt_group(qkv, g, batch=B, seq=T) for g in range(N_GROUPS)]
            xp = merge_out_proj(og[0], og[1], og[2], wo, xp, tm=_row_tile(NP, 512))
            qkv3 = qkv.reshape(B, T, QKV_COLS)
            for g, (win, _) in enumerate(ATT_GROUPS):
                keep = min(win, T)
                kvg = qkv3[:, T - keep:, g * GROUP_COLS + HD_ALL:(g + 1) * GROUP_COLS]
                p_kv[g].append(kvg.astype(F32).reshape(B, keep, 2, ATT_HEADS, HEAD_DIM))
            qkvs = norm_qkv(xs, norm_mix[i], wqkv, cos_s, sin_s, tm=tms, out_dtype=F32)
            qkvs3 = jnp.swapaxes(qkvs.reshape(S, DB, QKV_COLS), 0, 1)
            nl = cache_kv_w128.shape[0]
            caches = [c.reshape(nl * DB, ATT_BLOCK, c.shape[2] // ATT_BLOCK, 2, ATT_HEADS, HEAD_DIM)
                      for c in (cache_kv_w128, cache_kv_w512, cache_kv_w2048)]
            os_ = attn_sample(qkvs3.reshape(DB, S, QKV_TILES, LANES), *caches, j, db=DB, s_len=S)
            xs = out_proj(jnp.swapaxes(os_.reshape(DB, S, HD_ALL), 0, 1).reshape(NS, HD_ALL), wo, zeros_d, xs, tm=tms)
            for g in range(N_GROUPS):
                kvg = qkvs3[:, :, g * GROUP_COLS + HD_ALL:(g + 1) * GROUP_COLS]
                s_kv[g].append(kvg.reshape(DB, S, 2, ATT_HEADS, HEAD_DIM))
        final = i == depth - 1
        w1 = ffn_w1[i].astype(BF16)
        w2 = ffn_w2[i].astype(BF16)
        xp = ffn(xp, norm_ffn[i], w1, w2, norm_final, tm=tmp, tf=512, final=final)
        xs = ffn(xs, norm_ffn[i], w1, w2, norm_final, tm=tms, tf=512, final=final)

    return (xp.reshape(B, T, D), jnp.swapaxes(xs.reshape(S, DB, D), 0, 1),
            jnp.stack(p_lru_conv), jnp.stack(p_lru_h), jnp.stack(p_cm_conv),
            jnp.stack(p_kv[0]), jnp.stack(p_kv[1]), jnp.stack(p_kv[2]),
            jnp.stack(s_lru_conv), jnp.stack(s_lru_h), jnp.stack(s_cm_conv),
            jnp.stack(s_kv[0]), jnp.stack(s_kv[1]), jnp.stack(s_kv[2]))
```

```python
import functools

import numpy as np
import jax
import jax.numpy as jnp
from jax import lax
from jax.experimental import pallas as pl
from jax.experimental.pallas import tpu as pltpu

F32 = jnp.float32
BF16 = jnp.bfloat16

LANES = 128
D_MODEL = 1024
D_RNN = 1408
LRU_BLOCKS = 16
LRU_BLOCK_DIM = D_RNN // LRU_BLOCKS
LRU_CONV_W = 4
LRU_C = 8.0
CM_KERNEL = 31
ATT_GROUPS = ((128, 1), (512, 4), (2048, 16))
N_GROUPS = 3
ATT_HEADS = 8
HEAD_DIM = 128
ATT_BLOCK = 128
ROPE_THETA = 10000.0
PAST_LEN = 2048
EPS = 1e-6
QKV_COLS = N_GROUPS * 3 * ATT_HEADS * HEAD_DIM
GROUP_COLS = 3 * ATT_HEADS * HEAD_DIM
HD_ALL = ATT_HEADS * HEAD_DIM
QKV_TILE = 512
N_CT = D_RNN // LANES
SCALE = HEAD_DIM ** -0.5
SCALE_LOG2E = SCALE * 1.4426950408889634
MIB = 1024 * 1024


def _cparams(sem, vmem_mib):
    return pltpu.CompilerParams(dimension_semantics=sem, vmem_limit_bytes=vmem_mib * MIB)


def _rms(x, g):
    ms = jnp.mean(x * x, axis=-1, keepdims=True)
    return (x * lax.rsqrt(ms + EPS)) * g


def _sigmoid(x):
    return 1.0 / (1.0 + jnp.exp(-x))


def _softplus(x):
    return jnp.maximum(x, 0.0) + jnp.log1p(jnp.exp(-jnp.abs(x)))


def _gelu_tanh(x):
    return x * (0.5 * (1.0 + jnp.tanh(0.7978845608028654 * (x + 0.044715 * (x * x * x)))))


def _norm_proj_gelu_kernel(x_ref, g_ref, w_ref, o_ref, *, half, rc):
    for r0 in range(0, x_ref.shape[0], rc):
        xn = _rms(x_ref[r0:r0 + rc, :], g_ref[...]).astype(BF16)
        o_ref[r0:r0 + rc, 0:half] = _gelu_tanh(jnp.dot(xn, w_ref[:, 0:half], preferred_element_type=F32))
        o_ref[r0:r0 + rc, half:2 * half] = jnp.dot(xn, w_ref[:, half:2 * half], preferred_element_type=F32)


def norm_proj_gelu(x, g, w, *, tm):
    n, d = x.shape
    nout = w.shape[1]
    return pl.pallas_call(
        functools.partial(_norm_proj_gelu_kernel, half=nout // 2, rc=min(tm, 256)),
        grid=(n // tm,),
        in_specs=[
            pl.BlockSpec((tm, d), lambda i: (i, 0)),
            pl.BlockSpec((1, d), lambda i: (0, 0)),
            pl.BlockSpec((d, nout), lambda i: (0, 0)),
        ],
        out_specs=pl.BlockSpec((tm, nout), lambda i: (i, 0)),
        out_shape=jax.ShapeDtypeStruct((n, nout), F32),
        compiler_params=_cparams(("parallel",), 48),
        name="norm_proj_gelu",
    )(x, g.reshape(1, d), w)


def _norm_glu_kernel(x_ref, g_ref, wa_ref, wg_ref, ba_ref, bg_ref, o_ref, xn_ref):
    @pl.when(pl.program_id(1) == 0)
    def _():
        xn_ref[...] = _rms(x_ref[...], g_ref[...]).astype(BF16)

    xn = xn_ref[...]
    a = jnp.dot(xn, wa_ref[...], preferred_element_type=F32) + ba_ref[...]
    gate = jnp.dot(xn, wg_ref[...], preferred_element_type=F32) + bg_ref[...]
    o_ref[...] = a * _sigmoid(gate)


def norm_glu(x, g, w, b, *, tm, tn):
    n, d = x.shape
    half = w.shape[1] // 2
    nb = half // tn
    b2 = b.reshape(1, 2 * half)
    return pl.pallas_call(
        _norm_glu_kernel,
        grid=(n // tm, nb),
        in_specs=[
            pl.BlockSpec((tm, d), lambda i, j: (i, 0)),
            pl.BlockSpec((1, d), lambda i, j: (0, 0)),
            pl.BlockSpec((d, tn), lambda i, j: (0, j)),
            pl.BlockSpec((d, tn), lambda i, j: (0, j + nb)),
            pl.BlockSpec((1, tn), lambda i, j: (0, j)),
            pl.BlockSpec((1, tn), lambda i, j: (0, j + nb)),
        ],
        out_specs=pl.BlockSpec((tm, tn), lambda i, j: (i, j)),
        out_shape=jax.ShapeDtypeStruct((n, half), F32),
        scratch_shapes=[pltpu.VMEM((tm, d), BF16)],
        compiler_params=_cparams(("parallel", "arbitrary"), 48),
        name="norm_glu",
    )(x, g.reshape(1, d), w, w, b2, b2)


def _norm_qkv_kernel(x_ref, g_ref, w_ref, cos_ref, sin_ref, *rest, permute):
    if permute:
        p_ref, o_ref, xn_ref = rest
    else:
        o_ref, xn_ref = rest
    j = pl.program_id(1)

    @pl.when(j == 0)
    def _():
        xn = _rms(x_ref[...], g_ref[...]).astype(BF16)
        xn_ref[0] = xn
        if permute:
            for g in range(1, N_GROUPS):
                xn_ref[g] = jnp.dot(p_ref[g - 1], xn, preferred_element_type=F32).astype(BF16)

    xn = xn_ref[j] if permute else xn_ref[0]
    cos = cos_ref[0]
    sin = sin_ref[0]
    for part in range(3):
        acc = jnp.dot(xn, w_ref[:, part * HD_ALL:(part + 1) * HD_ALL], preferred_element_type=F32)
        for h in range(ATT_HEADS):
            hs = slice(h * HEAD_DIM, (h + 1) * HEAD_DIM)
            xh = acc[:, hs]
            if part < 2:
                xh = xh * cos + pltpu.roll(xh, HEAD_DIM // 2, 1) * sin
            o_ref[:, part * HD_ALL + h * HEAD_DIM:part * HD_ALL + (h + 1) * HEAD_DIM] = xh.astype(o_ref.dtype)


def norm_qkv(x, g, w, cos, sin, perms, *, tm, out_dtype):
    n, d = x.shape
    period = cos.shape[1]
    nper = period // tm
    permute = perms is not None
    tab = (lambda i, j: (j, i % nper, 0)) if permute else (lambda i, j: (0, i % nper, 0))
    in_specs = [
        pl.BlockSpec((tm, d), lambda i, j: (i, 0)),
        pl.BlockSpec((1, d), lambda i, j: (0, 0)),
        pl.BlockSpec((d, GROUP_COLS), lambda i, j: (0, j)),
        pl.BlockSpec((1, tm, HEAD_DIM), tab),
        pl.BlockSpec((1, tm, HEAD_DIM), tab),
    ]
    args = [x, g.reshape(1, d), w, cos, sin]
    if permute:
        in_specs.append(pl.BlockSpec((N_GROUPS - 1, tm, tm), lambda i, j: (0, 0, 0)))
        args.append(perms)
    return pl.pallas_call(
        functools.partial(_norm_qkv_kernel, permute=permute),
        grid=(n // tm, N_GROUPS),
        in_specs=in_specs,
        out_specs=pl.BlockSpec((tm, GROUP_COLS), lambda i, j: (i, j)),
        out_shape=jax.ShapeDtypeStruct((n, QKV_COLS), out_dtype),
        scratch_shapes=[pltpu.VMEM((N_GROUPS if permute else 1, tm, d), BF16)],
        compiler_params=_cparams(("parallel", "arbitrary"), 48),
        name="norm_qkv",
    )(*args)


def _ffn_kernel(x_ref, g_ref, w1_ref, w2_ref, gf_ref, o_ref, xn_ref, acc_ref, *, final):
    f = pl.program_id(1)

    @pl.when(f == 0)
    def _():
        xn_ref[...] = _rms(x_ref[...], g_ref[...]).astype(BF16)
        acc_ref[...] = jnp.zeros_like(acc_ref)

    h = jnp.dot(xn_ref[...], w1_ref[...], preferred_element_type=F32)
    h = jnp.maximum(h, 0.0)
    h = (h * h).astype(BF16)
    acc_ref[...] += jnp.dot(h, w2_ref[...], preferred_element_type=F32)

    @pl.when(f == pl.num_programs(1) - 1)
    def _():
        y = x_ref[...] + acc_ref[...]
        if final:
            y = _rms(y, gf_ref[...])
        o_ref[...] = y


def ffn(x, g, w1, w2, gf, *, tm, tf, final):
    n, d = x.shape
    dff = w1.shape[1]
    return pl.pallas_call(
        functools.partial(_ffn_kernel, final=final),
        grid=(n // tm, dff // tf),
        in_specs=[
            pl.BlockSpec((tm, d), lambda i, f: (i, 0)),
            pl.BlockSpec((1, d), lambda i, f: (0, 0)),
            pl.BlockSpec((d, tf), lambda i, f: (0, f)),
            pl.BlockSpec((tf, d), lambda i, f: (f, 0)),
            pl.BlockSpec((1, d), lambda i, f: (0, 0)),
        ],
        out_specs=pl.BlockSpec((tm, d), lambda i, f: (i, 0)),
        out_shape=jax.ShapeDtypeStruct((n, d), F32),
        scratch_shapes=[pltpu.VMEM((tm, d), BF16), pltpu.VMEM((tm, d), F32)],
        compiler_params=_cparams(("parallel", "arbitrary"), 48),
        name="ffn",
    )(x, g.reshape(1, d), w1, w2, gf.reshape(1, d))


def _out_proj_kernel(a_ref, w_ref, b_ref, r_ref, o_ref):
    y = jnp.dot(a_ref[...].astype(BF16), w_ref[...], preferred_element_type=F32)
    o_ref[...] = r_ref[...] + (y + b_ref[...])


def out_proj(a, w, b, resid, *, tm):
    n, k = a.shape
    d = w.shape[1]
    return pl.pallas_call(
        _out_proj_kernel,
        grid=(n // tm,),
        in_specs=[
            pl.BlockSpec((tm, k), lambda i: (i, 0)),
            pl.BlockSpec((k, d), lambda i: (0, 0)),
            pl.BlockSpec((1, d), lambda i: (0, 0)),
            pl.BlockSpec((tm, d), lambda i: (i, 0)),
        ],
        out_specs=pl.BlockSpec((tm, d), lambda i: (i, 0)),
        out_shape=jax.ShapeDtypeStruct((n, d), F32),
        compiler_params=_cparams(("parallel",), 48),
        name="out_proj",
    )(a, w, b.reshape(1, d), resid)


def _unpermute_f32(pt, x):
    hi = x.astype(BF16)
    r1 = x - hi.astype(F32)
    mid = r1.astype(BF16)
    lo = (r1 - mid.astype(F32)).astype(BF16)
    out = jnp.dot(pt, hi, preferred_element_type=F32)
    out = out + jnp.dot(pt, mid, preferred_element_type=F32)
    return out + jnp.dot(pt, lo, preferred_element_type=F32)


def _merge_out_proj_kernel(o0_ref, o1_ref, o2_ref, l0_ref, l1_ref, l2_ref, pt_ref, w_ref, r_ref, o_ref, m_scr):
    outs = [o0_ref[...].astype(F32),
            jnp.dot(pt_ref[0], o1_ref[...], preferred_element_type=F32),
            jnp.dot(pt_ref[1], o2_ref[...], preferred_element_type=F32)]
    lses = [l0_ref[:, 0:ATT_HEADS],
            _unpermute_f32(pt_ref[0], l1_ref[...])[:, 0:ATT_HEADS],
            _unpermute_f32(pt_ref[1], l2_ref[...])[:, 0:ATT_HEADS]]
    mx = jnp.maximum(jnp.maximum(lses[0], lses[1]), lses[2])
    es = [jnp.exp(l - mx) for l in lses]
    inv = 1.0 / (es[0] + es[1] + es[2])
    ws = [e * inv for e in es]
    for h in range(ATT_HEADS):
        sl = slice(h * HEAD_DIM, (h + 1) * HEAD_DIM)
        acc = ws[0][:, h:h + 1] * outs[0][:, sl]
        acc = acc + ws[1][:, h:h + 1] * outs[1][:, sl]
        acc = acc + ws[2][:, h:h + 1] * outs[2][:, sl]
        m_scr[:, sl] = acc.astype(BF16)
    o_ref[...] = r_ref[...] + jnp.dot(m_scr[...], w_ref[...], preferred_element_type=F32)


def merge_out_proj(os_, ls_, perms_t, w, resid, *, tm):
    n = resid.shape[0]
    d = w.shape[1]
    ospec = pl.BlockSpec((tm, HD_ALL), lambda i: (i, 0))
    lspec = pl.BlockSpec((tm, LANES), lambda i: (i, 0))
    return pl.pallas_call(
        _merge_out_proj_kernel,
        grid=(n // tm,),
        in_specs=[ospec, ospec, ospec, lspec, lspec, lspec,
                  pl.BlockSpec((N_GROUPS - 1, tm, tm), lambda i: (0, 0, 0)),
                  pl.BlockSpec((HD_ALL, d), lambda i: (0, 0)),
                  pl.BlockSpec((tm, d), lambda i: (i, 0))],
        out_specs=pl.BlockSpec((tm, d), lambda i: (i, 0)),
        out_shape=jax.ShapeDtypeStruct((n, d), F32),
        scratch_shapes=[pltpu.VMEM((tm, HD_ALL), BF16)],
        compiler_params=_cparams(("parallel",), 48),
        name="merge_out_proj",
    )(*os_, *ls_, perms_t, w, resid)


def _band_slices(c):
    lo = max(c - 1, 0)
    hi = min(c + 2, N_CT)
    return (lo * LANES, hi * LANES), ((lo - (c - 1)) * LANES, (hi - (c - 1)) * LANES)


def _lru_ab(za, zx, u, sp):
    r = _sigmoid(za)
    i = _sigmoid(zx)
    log_a = (-LRU_C * r) * sp
    a = jnp.exp(log_a)
    b = jnp.sqrt(-jnp.tanh(log_a) * (a * a + 1.0)) * (i * u)
    return a, b


def _lru_gates(xg_tile_fn, wb_ref, c, u, gab, gxb, sp):
    (xlo, xhi), (rlo, rhi) = _band_slices(c)
    z = jnp.dot(xg_tile_fn(xlo, xhi), wb_ref[c, rlo:rhi, :], preferred_element_type=F32)
    return _lru_ab(z[:, :LANES] + gab, z[:, LANES:] + gxb, u, sp)


def _lru_mix_prompt_kernel(gu_ref, cw_ref, cb_ref, wb_ref, gab_ref, gxb_ref, lam_ref,
                           hg_ref, cs_ref, hl_ref, ubuf, cv_scr, xg_scr, hc_scr, *, tt):
    t = pl.program_id(1)
    nt = pl.num_programs(1)
    C = D_RNN
    RB = 64
    SUB = 8

    @pl.when(t == 0)
    def _():
        ubuf[0:8, :] = jnp.zeros((8, C), F32)
        hc_scr[...] = jnp.zeros((SUB, C), F32)

    ubuf[8:8 + tt, :] = gu_ref[:, C:2 * C]

    for c in range(N_CT):
        cs = slice(c * LANES, (c + 1) * LANES)
        for r0 in range(0, tt, RB):
            acc = cw_ref[0:1, cs] * ubuf[r0 + 5:r0 + 5 + RB, cs]
            for k in range(1, LRU_CONV_W):
                acc = acc + cw_ref[k:k + 1, cs] * ubuf[r0 + 5 + k:r0 + 5 + k + RB, cs]
            acc = acc + cb_ref[:, cs]
            cv_scr[r0:r0 + RB, cs] = acc
            xg_scr[r0:r0 + RB, cs] = acc.astype(BF16)

    @pl.when(t == nt - 1)
    def _():
        cs_ref[0] = ubuf[tt + 5:tt + 8, :]

    ubuf[0:8, :] = ubuf[tt:tt + 8, :]

    sub = lax.broadcasted_iota(jnp.int32, (SUB, LANES), 0)
    keep = [sub >= d for d in (1, 2, 4)]
    sp = _softplus(-lam_ref[...])
    for c in range(N_CT):
        cs = slice(c * LANES, (c + 1) * LANES)
        (xlo, xhi), (rlo, rhi) = _band_slices(c)
        z = jnp.dot(xg_scr[:, xlo:xhi], wb_ref[c, rlo:rhi, :], preferred_element_type=F32)
        hin = hc_scr[:, cs]
        for r0 in range(0, tt, 2 * SUB):
            hs = []
            for r1 in (r0, r0 + SUB):
                a, b = _lru_ab(z[r1:r1 + SUB, :LANES] + gab_ref[:, cs], z[r1:r1 + SUB, LANES:] + gxb_ref[:, cs],
                               cv_scr[r1:r1 + SUB, cs], sp[:, cs])
                for d, kp in zip((1, 2, 4), keep):
                    a_sh = jnp.where(kp, pltpu.roll(a, d, 0), 1.0)
                    b_sh = jnp.where(kp, pltpu.roll(b, d, 0), 0.0)
                    b = a * b_sh + b
                    a = a * a_sh
                h = b + a * hin
                hin = jnp.broadcast_to(h[SUB - 1:SUB, :], (SUB, LANES))
                hs.append(h)
            h2 = jnp.concatenate(hs, axis=0)
            hg_ref[r0:r0 + 2 * SUB, cs] = (h2 * gu_ref[r0:r0 + 2 * SUB, cs]).astype(hg_ref.dtype)
        hc_scr[:, cs] = hin

    @pl.when(t == nt - 1)
    def _():
        hl_ref[0] = hc_scr[0:1, :]


def lru_mix_prompt(gu, cw, cb, wb, gab, gxb, lam, *, batch, seq, tt):
    C = D_RNN
    nt = seq // tt
    row = lambda v: v.reshape(1, C)
    full2 = lambda shape: pl.BlockSpec(shape, lambda b, t: (0,) * len(shape))
    return pl.pallas_call(
        functools.partial(_lru_mix_prompt_kernel, tt=tt),
        grid=(batch, nt),
        in_specs=[
            pl.BlockSpec((tt, 2 * C), lambda b, t: (b * nt + t, 0)),
            full2((LRU_CONV_W, C)), full2((1, C)), full2((N_CT, 3 * LANES, 2 * LANES)),
            full2((1, C)), full2((1, C)), full2((1, C)),
        ],
        out_specs=[
            pl.BlockSpec((tt, C), lambda b, t: (b * nt + t, 0)),
            pl.BlockSpec((1, LRU_CONV_W - 1, C), lambda b, t: (b, 0, 0)),
            pl.BlockSpec((1, 1, C), lambda b, t: (b, 0, 0)),
        ],
        out_shape=[
            jax.ShapeDtypeStruct((batch * seq, C), BF16),
            jax.ShapeDtypeStruct((batch, LRU_CONV_W - 1, C), F32),
            jax.ShapeDtypeStruct((batch, 1, C), F32),
        ],
        scratch_shapes=[
            pltpu.VMEM((tt + 8, C), F32), pltpu.VMEM((tt, C), F32), pltpu.VMEM((tt, C), BF16),
            pltpu.VMEM((8, C), F32),
        ],
        compiler_params=_cparams(("arbitrary", "arbitrary"), 48),
        name="lru_mix_prompt",
    )(gu, cw, row(cb), wb, row(gab), row(gxb), row(lam))


def _lru_mix_sample_kernel(gu_ref, cst_ref, hp_ref, cw_ref, cb_ref, wb_ref, gab_ref, gxb_ref, lam_ref,
                           hg_ref, cso_ref, hl_ref, cv_scr, xg_scr, *, db, s_len):
    del db
    C = D_RNN
    W1 = LRU_CONV_W - 1

    def ext(j, cs):
        if j < W1:
            return cst_ref[j, :, cs]
        return gu_ref[j - W1, :, C + cs.start:C + cs.stop]

    for c in range(N_CT):
        cs = slice(c * LANES, (c + 1) * LANES)
        for s in range(s_len):
            acc = cw_ref[0:1, cs] * ext(s, cs)
            for k in range(1, LRU_CONV_W):
                acc = acc + cw_ref[k:k + 1, cs] * ext(s + k, cs)
            acc = acc + cb_ref[:, cs]
            cv_scr[s, :, cs] = acc
            xg_scr[s, :, cs] = acc.astype(BF16)
        for k in range(W1):
            cso_ref[k, :, cs] = ext(s_len + k, cs)

    sp = _softplus(-lam_ref[...])
    for c in range(N_CT):
        cs = slice(c * LANES, (c + 1) * LANES)
        h = hp_ref[:, cs]
        for s in range(s_len):
            a, b = _lru_gates(lambda lo, hi: xg_scr[s, :, lo:hi], wb_ref, c, cv_scr[s, :, cs],
                              gab_ref[:, cs], gxb_ref[:, cs], sp[:, cs])
            h = a * h + b
            hg_ref[s, :, cs] = h * gu_ref[s, :, cs]
        hl_ref[:, cs] = h


def lru_mix_sample(gu, cst, hp, cw, cb, wb, gab, gxb, lam, *, db, s_len):
    C = D_RNN
    row = lambda v: v.reshape(1, C)
    return pl.pallas_call(
        functools.partial(_lru_mix_sample_kernel, db=db, s_len=s_len),
        out_shape=[
            jax.ShapeDtypeStruct((s_len, db, C), F32),
            jax.ShapeDtypeStruct((LRU_CONV_W - 1, db, C), F32),
            jax.ShapeDtypeStruct((db, C), F32),
        ],
        scratch_shapes=[pltpu.VMEM((s_len, db, C), F32), pltpu.VMEM((s_len, db, C), BF16)],
        compiler_params=pltpu.CompilerParams(vmem_limit_bytes=48 * MIB),
        name="lru_mix_sample",
    )(gu, cst, hp, cw, row(cb), wb, row(gab), row(gxb), row(lam))


def _ln_silu(y, g, b):
    mu = jnp.mean(y, axis=-1, keepdims=True)
    yc = y - mu
    var = jnp.mean(yc * yc, axis=-1, keepdims=True)
    z = yc * lax.rsqrt(var + EPS) * g + b
    return z * _sigmoid(z)


def _cm_mix_prompt_kernel(u_ref, dw_ref, db_ref, lg_ref, lb_ref, o_ref, cs_ref, ubuf, cv_scr, *, tt):
    t = pl.program_id(1)
    nt = pl.num_programs(1)
    C = D_MODEL
    RB = 64
    HALO = 32
    OFF = HALO - (CM_KERNEL - 1)

    @pl.when(t == 0)
    def _():
        ubuf[0:HALO, :] = jnp.zeros((HALO, C), F32)

    ubuf[HALO:HALO + tt, :] = u_ref[...]

    WIN = RB + HALO
    for c in range(C // LANES):
        cs = slice(c * LANES, (c + 1) * LANES)
        for r0 in range(0, tt, RB):
            x = ubuf[r0:r0 + WIN, cs]
            acc = None
            for r in range(8):
                xr = x if r == 0 else pltpu.roll(x, WIN - r, 0)
                for o in range(OFF, OFF + CM_KERNEL):
                    if o % 8 != r:
                        continue
                    term = dw_ref[o - OFF:o - OFF + 1, cs] * xr[o - r:o - r + RB, :]
                    acc = term if acc is None else acc + term
            cv_scr[r0:r0 + RB, cs] = acc + db_ref[:, cs]

    @pl.when(t == nt - 1)
    def _():
        cs_ref[0] = ubuf[tt + OFF:tt + HALO, :]

    ubuf[0:HALO, :] = ubuf[tt:tt + HALO, :]

    for r0 in range(0, tt, RB):
        o_ref[r0:r0 + RB, :] = _ln_silu(cv_scr[r0:r0 + RB, :], lg_ref[...], lb_ref[...]).astype(o_ref.dtype)


def cm_mix_prompt(u, dw, db_, lg, lb, *, batch, seq, tt):
    C = D_MODEL
    nt = seq // tt
    row = lambda v: v.reshape(1, C)
    full2 = lambda shape: pl.BlockSpec(shape, lambda b, t: (0,) * len(shape))
    return pl.pallas_call(
        functools.partial(_cm_mix_prompt_kernel, tt=tt),
        grid=(batch, nt),
        in_specs=[
            pl.BlockSpec((tt, C), lambda b, t: (b * nt + t, 0)),
            full2((CM_KERNEL, C)), full2((1, C)), full2((1, C)), full2((1, C)),
        ],
        out_specs=[
            pl.BlockSpec((tt, C), lambda b, t: (b * nt + t, 0)),
            pl.BlockSpec((1, CM_KERNEL - 1, C), lambda b, t: (b, 0, 0)),
        ],
        out_shape=[
            jax.ShapeDtypeStruct((batch * seq, C), BF16),
            jax.ShapeDtypeStruct((batch, CM_KERNEL - 1, C), F32),
        ],
        scratch_shapes=[pltpu.VMEM((tt + 32, C), F32), pltpu.VMEM((tt, C), F32)],
        compiler_params=_cparams(("arbitrary", "arbitrary"), 48),
        name="cm_mix_prompt",
    )(u, dw, row(db_), row(lg), row(lb))


def _cm_mix_sample_kernel(u_ref, st_ref, dw_ref, db_ref, lg_ref, lb_ref, o_ref, so_ref, cv_scr, *, nb, s_len):
    del nb
    C = D_MODEL
    W1 = CM_KERNEL - 1

    def ext(j, cs):
        if j < W1:
            return st_ref[j, :, cs]
        return u_ref[j - W1, :, cs]

    for c in range(C // LANES):
        cs = slice(c * LANES, (c + 1) * LANES)
        for s in range(s_len):
            acc = dw_ref[0:1, cs] * ext(s, cs)
            for k in range(1, CM_KERNEL):
                acc = acc + dw_ref[k:k + 1, cs] * ext(s + k, cs)
            cv_scr[s, :, cs] = acc + db_ref[:, cs]
        for j in range(W1):
            so_ref[j, :, cs] = ext(j + s_len, cs)

    for s in range(s_len):
        o_ref[s] = _ln_silu(cv_scr[s], lg_ref[...], lb_ref[...])


def cm_mix_sample(u, st, dw, db_, lg, lb, *, db, s_len, nb):
    C = D_MODEL
    W1 = CM_KERNEL - 1
    row = lambda v: v.reshape(1, C)
    full1 = lambda shape: pl.BlockSpec(shape, lambda i: (0,) * len(shape))
    return pl.pallas_call(
        functools.partial(_cm_mix_sample_kernel, nb=nb, s_len=s_len),
        grid=(db // nb,),
        in_specs=[
            pl.BlockSpec((s_len, nb, C), lambda i: (0, i, 0)),
            pl.BlockSpec((W1, nb, C), lambda i: (0, i, 0)),
            full1((CM_KERNEL, C)), full1((1, C)), full1((1, C)), full1((1, C)),
        ],
        out_specs=[
            pl.BlockSpec((s_len, nb, C), lambda i: (0, i, 0)),
            pl.BlockSpec((W1, nb, C), lambda i: (0, i, 0)),
        ],
        out_shape=[
            jax.ShapeDtypeStruct((s_len, db, C), F32),
            jax.ShapeDtypeStruct((W1, db, C), F32),
        ],
        scratch_shapes=[pltpu.VMEM((s_len, nb, C), F32)],
        compiler_params=_cparams(("parallel",), 48),
        name="cm_mix_sample",
    )(u, st, dw, row(db_), row(lg), row(lb))


def _attn_prompt_kernel(q_ref, k_ref, v_ref, kp_ref, vp_ref, o_ref, l_ref, qs, kext, vext, os, ls, s_scr, *, ch):
    n = pl.program_id(2)
    qs[...] = q_ref[:, 0].reshape(ch, HD_ALL)
    kext[0:ATT_BLOCK, :] = kp_ref[:, 0].reshape(ATT_BLOCK, HD_ALL)
    kext[ATT_BLOCK:ATT_BLOCK + ch, :] = k_ref[:, 0].reshape(ch, HD_ALL)
    vext[0:ATT_BLOCK, :] = vp_ref[:, 0].reshape(ATT_BLOCK, HD_ALL)
    vext[ATT_BLOCK:ATT_BLOCK + ch, :] = v_ref[:, 0].reshape(ch, HD_ALL)
    qi = lax.broadcasted_iota(jnp.int32, (ATT_BLOCK, 2 * ATT_BLOCK), 0)
    kj = lax.broadcasted_iota(jnp.int32, (ATT_BLOCK, 2 * ATT_BLOCK), 1)
    band = jnp.logical_and(kj >= qi, kj <= qi + ATT_BLOCK)
    band_first = jnp.logical_and(band, jnp.logical_or(kj >= ATT_BLOCK, n > 0))
    lane = lax.broadcasted_iota(jnp.int32, (ATT_BLOCK, LANES), 1)
    dn = (((1,), (1,)), ((), ()))
    for i in range(ch // ATT_BLOCK):
        rows = slice(i * ATT_BLOCK, (i + 1) * ATT_BLOCK)
        krows = slice(i * ATT_BLOCK, (i + 2) * ATT_BLOCK)
        mask = band_first if i == 0 else band
        for h in range(ATT_HEADS):
            hs = slice(h * HEAD_DIM, (h + 1) * HEAD_DIM)
            s_scr[i % 2, h] = lax.dot_general(qs[rows, hs], kext[krows, hs], dn, preferred_element_type=F32)
        lse_tile = jnp.zeros((ATT_BLOCK, LANES), F32)
        for h in range(ATT_HEADS):
            hs = slice(h * HEAD_DIM, (h + 1) * HEAD_DIM)
            s = jnp.where(mask, s_scr[i % 2, h], -jnp.inf)
            m = jnp.max(jnp.maximum(s[:, :ATT_BLOCK], s[:, ATT_BLOCK:]), axis=-1, keepdims=True)
            p = jnp.exp2((s - m) * SCALE_LOG2E)
            l = jnp.sum(p[:, :ATT_BLOCK] + p[:, ATT_BLOCK:], axis=-1, keepdims=True)
            o = jnp.dot(p.astype(BF16), vext[krows, hs], preferred_element_type=F32)
            os[rows, hs] = (o * (1.0 / l)).astype(os.dtype)
            lse_tile = jnp.where(lane == h, m * SCALE + jnp.log(l), lse_tile)
        ls[rows, :] = lse_tile
    o_ref[:, 0] = os[...].reshape(o_ref.shape[0], o_ref.shape[2], HD_ALL)
    l_ref[:, 0] = ls[...].reshape(l_ref.shape[0], l_ref.shape[2], LANES)


def attn_prompt_group(qkv, g, *, batch, seq):
    win, dil = ATT_GROUPS[g]
    m = seq // dil
    ch = min(m, 512)
    nch = m // ch
    bpc = ch // ATT_BLOCK
    nblk = m // ATT_BLOCK
    rpt = QKV_TILE // dil
    ntc = ch // rpt
    prpt = min(rpt, ATT_BLOCK)
    ntiles = batch * seq // QKV_TILE
    cur_view = qkv.reshape(ntiles, dil, rpt, QKV_COLS)
    prev_view = qkv.reshape(ntiles * (rpt // prpt), dil, prpt, QKV_COLS)
    ncb = QKV_COLS // HD_ALL
    cq, ck, cv = 3 * g, 3 * g + 1, 3 * g + 2

    def cur(cb):
        return pl.BlockSpec((ntc, 1, rpt, HD_ALL), lambda b, r, n: (b * nch + n, r, 0, cb))

    def prev(cb):
        return pl.BlockSpec((ATT_BLOCK // prpt, 1, prpt, HD_ALL),
                            lambda b, r, n: (b * nblk + jnp.maximum(n * bpc - 1, 0), r, 0, cb))

    o, l = pl.pallas_call(
        functools.partial(_attn_prompt_kernel, ch=ch),
        grid=(batch, dil, nch),
        in_specs=[cur(cq), cur(ck), cur(cv), prev(ck), prev(cv)],
        out_specs=[pl.BlockSpec((ntc, 1, rpt, HD_ALL), lambda b, r, n: (b * nch + n, r, 0, 0)),
                   pl.BlockSpec((ntc, 1, rpt, LANES), lambda b, r, n: (b * nch + n, r, 0, 0))],
        out_shape=[jax.ShapeDtypeStruct((ntiles, dil, rpt, HD_ALL), BF16),
                   jax.ShapeDtypeStruct((ntiles, dil, rpt, LANES), F32)],
        scratch_shapes=[pltpu.VMEM((ch, HD_ALL), BF16),
                        pltpu.VMEM((ATT_BLOCK + ch, HD_ALL), BF16), pltpu.VMEM((ATT_BLOCK + ch, HD_ALL), BF16),
                        pltpu.VMEM((ch, HD_ALL), BF16), pltpu.VMEM((ch, LANES), F32),
                        pltpu.VMEM((2, ATT_HEADS, ATT_BLOCK, 2 * ATT_BLOCK), F32)],
        compiler_params=_cparams(("parallel", "parallel", "arbitrary"), 48),
        name=f"attn_prompt_g{g}",
    )(cur_view, cur_view, cur_view, prev_view, prev_view)
    return o.reshape(batch * seq, HD_ALL), l.reshape(batch * seq, LANES)


QKV_TILES = QKV_COLS // LANES
GROUP_TILES = GROUP_COLS // LANES


def _attn_sample_kernel(qkv_ref, c0_ref, c1_ref, c2_ref, o_ref, s_scr, p_scr, *, s_len):
    nk = ATT_BLOCK * ATT_HEADS
    caches = (c0_ref, c1_ref, c2_ref)
    col = lax.broadcasted_iota(jnp.int32, (ATT_HEADS, nk), 1)
    rowh = lax.broadcasted_iota(jnp.int32, (ATT_HEADS, nk), 0)
    own = (col & (ATT_HEADS - 1)) == rowh
    key = col >> 3
    dn = (((1,), (1,)), ((), ()))

    def kvflat(g, s, which):
        r = 0 if g == 0 else s
        return caches[g][0, :, r, which].reshape(nk, HEAD_DIM).astype(BF16)

    def tile(s, g, which):
        lo = g * GROUP_TILES + which * ATT_HEADS
        return qkv_ref[0, s, lo:lo + ATT_HEADS, :]

    def rnd(x):
        return x.astype(BF16).astype(F32)

    q0 = qkv_ref[0, :, 0:ATT_HEADS, :].reshape(s_len * ATT_HEADS, HEAD_DIM).astype(BF16)
    s0 = lax.dot_general(q0, kvflat(0, 0, 0), dn, preferred_element_type=F32)
    s_scr[0:s_len] = s0.reshape(s_len, ATT_HEADS, nk)
    for g in range(1, N_GROUPS):
        for s in range(s_len):
            s_scr[g * s_len + s] = lax.dot_general(tile(s, g, 0).astype(BF16), kvflat(g, s, 0), dn,
                                                   preferred_element_type=F32)

    stats = {}
    for g in range(N_GROUPS):
        for s in range(s_len):
            idx = g * s_len + s
            valid = jnp.logical_and(own, key >= s) if g == 0 else own
            sc = jnp.where(valid, s_scr[idx], -jnp.inf)
            news = tuple(range(s + 1)) if g == 0 else (s,)
            qf = rnd(tile(s, g, 0))
            sn = [jnp.sum(qf * rnd(tile(j, g, 1)), axis=-1, keepdims=True) for j in news]
            m = jnp.max(sc, axis=-1, keepdims=True)
            for x in sn:
                m = jnp.maximum(m, x)
            p = jnp.exp2((sc - m) * SCALE_LOG2E)
            pn = [jnp.exp2((x - m) * SCALE_LOG2E) for x in sn]
            l = jnp.sum(p, axis=-1, keepdims=True)
            for x in pn:
                l = l + x
            p_scr[idx] = p
            stats[(g, s)] = (m, l, pn, news)

    outs = {}
    p0 = p_scr[0:s_len].reshape(s_len * ATT_HEADS, nk).astype(BF16)
    o0 = jnp.dot(p0, kvflat(0, 0, 1), preferred_element_type=F32).reshape(s_len, ATT_HEADS, HEAD_DIM)
    for g in range(N_GROUPS):
        for s in range(s_len):
            idx = g * s_len + s
            m, l, pn, news = stats[(g, s)]
            if g == 0:
                o = o0[s]
            else:
                o = jnp.dot(p_scr[idx].astype(BF16), kvflat(g, s, 1), preferred_element_type=F32)
            for x, j in zip(pn, news):
                o = o + rnd(x) * rnd(tile(j, g, 2))
            outs[(g, s)] = (o * (1.0 / l), m * SCALE + jnp.log(l))

    for s in range(s_len):
        lses = [outs[(g, s)][1] for g in range(N_GROUPS)]
        mx = jnp.maximum(jnp.maximum(lses[0], lses[1]), lses[2])
        es = [jnp.exp(x - mx) for x in lses]
        inv = 1.0 / (es[0] + es[1] + es[2])
        o_ref[0, s] = ((es[0] * inv) * outs[(0, s)][0] + (es[1] * inv) * outs[(1, s)][0]
                       + (es[2] * inv) * outs[(2, s)][0])


def attn_sample(qkv, c0, c1, c2, layer, *, db, s_len):
    nk = ATT_BLOCK * ATT_HEADS

    def cspec(c):
        r = min(c.shape[2], s_len)
        return pl.BlockSpec((1, ATT_BLOCK, r, 2, ATT_HEADS, HEAD_DIM), lambda b: (layer * db + b, 0, 0, 0, 0, 0))

    return pl.pallas_call(
        functools.partial(_attn_sample_kernel, s_len=s_len),
        grid=(db,),
        in_specs=[pl.BlockSpec((1, s_len, QKV_TILES, LANES), lambda b: (b, 0, 0, 0)),
                  cspec(c0), cspec(c1), cspec(c2)],
        out_specs=pl.BlockSpec((1, s_len, ATT_HEADS, HEAD_DIM), lambda b: (b, 0, 0, 0)),
        out_shape=jax.ShapeDtypeStruct((db, s_len, ATT_HEADS, HEAD_DIM), F32),
        scratch_shapes=[pltpu.VMEM((N_GROUPS * s_len, ATT_HEADS, nk), F32),
                        pltpu.VMEM((N_GROUPS * s_len, ATT_HEADS, nk), F32)],
        compiler_params=_cparams(("arbitrary",), 56),
        name="attn_sample",
    )(qkv, c0, c1, c2)


def _gate_band(ga_w, gx_w):
    eye = jnp.eye(LRU_BLOCKS, dtype=F32)

    def dense(w):
        d = jnp.einsum('ncd,nm->ncmd', w, eye).reshape(D_RNN, D_RNN)
        return jnp.pad(d, ((LANES, LANES), (0, 0)))

    da, dx = dense(ga_w), dense(gx_w)
    bands = [jnp.concatenate([da[c * LANES:(c + 3) * LANES, c * LANES:(c + 1) * LANES],
                              dx[c * LANES:(c + 3) * LANES, c * LANES:(c + 1) * LANES]], axis=1)
             for c in range(N_CT)]
    return jnp.stack(bands).astype(BF16)


def _tile_order(dil):
    rows = np.arange(QKV_TILE)
    rpt = QKV_TILE // dil
    return (rows % rpt) * dil + rows // rpt


def _perm_matrices():
    mats = []
    for _, dil in ATT_GROUPS[1:]:
        p = np.zeros((QKV_TILE, QKV_TILE), np.float32)
        p[np.arange(QKV_TILE), _tile_order(dil)] = 1.0
        mats.append(p)
    return np.stack(mats)


def _rope_tables(pos):
    half = HEAD_DIM // 2
    inv_freq = ROPE_THETA ** (-jnp.arange(half, dtype=F32) / half)
    ang = pos.astype(F32)[:, None] * inv_freq
    cos, sin = jnp.cos(ang), jnp.sin(ang)
    return jnp.concatenate([cos, cos], axis=-1), jnp.concatenate([-sin, sin], axis=-1)


def _row_tile(n, cap):
    t = cap
    while n % t:
        t //= 2
    return t


def kernel(x_prompt, x_sample, state_lru_conv, state_lru_h, state_cm_conv, cache_kv_w128, cache_kv_w512, cache_kv_w2048, norm_mix, norm_ffn, norm_final, lru_w_in, lru_conv_w, lru_conv_b, lru_gate_a_w, lru_gate_a_b, lru_gate_x_w, lru_gate_x_b, lru_lambda, lru_w_out, cm_w_pw1, cm_b_pw1, cm_dw_w, cm_dw_b, cm_ln_g, cm_ln_b, cm_w_pw2, cm_b_pw2, att_w_qkv, att_w_o, ffn_w1, ffn_w2):
    B, T, D = x_prompt.shape
    DB, S, _ = x_sample.shape
    depth = norm_mix.shape[0]
    NP, NS = B * T, DB * S
    assert D == D_MODEL and T % (ATT_GROUPS[-1][1] * ATT_BLOCK) == 0 and T % QKV_TILE == 0 and S == 4
    for c, (win, _) in zip((cache_kv_w128, cache_kv_w512, cache_kv_w2048), ATT_GROUPS):
        assert c.shape[2] == win
    tmp = _row_tile(NP, 1024)
    tms = _row_tile(NS, 512)
    zeros_d = jnp.zeros((D,), F32)

    xp = x_prompt.reshape(NP, D)
    xs = jnp.swapaxes(x_sample, 0, 1).reshape(NS, D)

    pos_p = np.arange(T).reshape(T // QKV_TILE, QKV_TILE)
    pos_g = np.stack([pos_p[:, _tile_order(dil)].reshape(T) for _, dil in ATT_GROUPS])
    cos_p, sin_p = _rope_tables(jnp.asarray(pos_g.reshape(-1)))
    cos_p, sin_p = cos_p.reshape(N_GROUPS, T, HEAD_DIM), sin_p.reshape(N_GROUPS, T, HEAD_DIM)
    pm = _perm_matrices()
    perms = jnp.asarray(pm, BF16)
    perms_t = jnp.asarray(np.swapaxes(pm, 1, 2), BF16)
    cos_s, sin_s = _rope_tables(PAST_LEN + jnp.arange(NS) // DB)

    p_lru_conv, p_lru_h, s_lru_conv, s_lru_h = [], [], [], []
    p_cm_conv, s_cm_conv = [], []
    p_kv = [[] for _ in ATT_GROUPS]
    s_kv = [[] for _ in ATT_GROUPS]

    for i in range(depth):
        m, j = i % 3, i // 3
        if m == 0:
            w_in = lru_w_in[j].astype(BF16)
            w_out = lru_w_out[j].astype(BF16)
            wb = _gate_band(lru_gate_a_w[j], lru_gate_x_w[j])
            lp = (lru_conv_w[j], lru_conv_b[j], wb, lru_gate_a_b[j], lru_gate_x_b[j], lru_lambda[j])
            gu = norm_proj_gelu(xp, norm_mix[i], w_in, tm=_row_tile(NP, 512))
            hg, cp, hp = lru_mix_prompt(gu, *lp, batch=B, seq=T, tt=256)
            xp = out_proj(hg, w_out, zeros_d, xp, tm=tmp)
            p_lru_conv.append(cp)
            p_lru_h.append(hp.reshape(B, D_RNN))
            gus = norm_proj_gelu(xs, norm_mix[i], w_in, tm=tms)
            hgs, cs, hs = lru_mix_sample(gus.reshape(S, DB, 2 * D_RNN), jnp.swapaxes(state_lru_conv[j], 0, 1),
                                         state_lru_h[j], *lp, db=DB, s_len=S)
            xs = out_proj(hgs.reshape(NS, D_RNN), w_out, zeros_d, xs, tm=tms)
            s_lru_conv.append(jnp.swapaxes(cs, 0, 1))
            s_lru_h.append(hs)
        elif m == 1:
            w1 = cm_w_pw1[j].astype(BF16)
            w2 = cm_w_pw2[j].astype(BF16)
            cp_ = (cm_dw_w[j], cm_dw_b[j], cm_ln_g[j], cm_ln_b[j])
            u = norm_glu(xp, norm_mix[i], w1, cm_b_pw1[j], tm=tmp, tn=512)
            y, cp = cm_mix_prompt(u, *cp_, batch=B, seq=T, tt=256)
            xp = out_proj(y, w2, cm_b_pw2[j], xp, tm=tmp)
            p_cm_conv.append(cp)
            us = norm_glu(xs, norm_mix[i], w1, cm_b_pw1[j], tm=tms, tn=512)
            nb = _row_tile(DB, 32)
            ys, cs = cm_mix_sample(us.reshape(S, DB, D), jnp.swapaxes(state_cm_conv[j], 0, 1), *cp_,
                                   db=DB, s_len=S, nb=nb)
            xs = out_proj(ys.reshape(NS, D), w2, cm_b_pw2[j], xs, tm=tms)
            s_cm_conv.append(jnp.swapaxes(cs, 0, 1))
        else:
            wqkv = att_w_qkv[j].astype(BF16)
            wo = att_w_o[j].astype(BF16)
            qkv = norm_qkv(xp, norm_mix[i], wqkv, cos_p, sin_p, perms, tm=QKV_TILE, out_dtype=BF16)
            og = [attn_prompt_group(qkv, g, batch=B, seq=T) for g in range(N_GROUPS)]
            xp = merge_out_proj([o for o, _ in og], [l for _, l in og], perms_t, wo, xp, tm=QKV_TILE)
            for g, (win, dil) in enumerate(ATT_GROUPS):
                keep = min(win, T)
                kvg = qkv.reshape(B, T, QKV_COLS)[:, T - keep:, g * GROUP_COLS + HD_ALL:(g + 1) * GROUP_COLS]
                if dil > 1:
                    kvg = kvg.reshape(B, keep // QKV_TILE, dil, QKV_TILE // dil, 2 * HD_ALL)
                    kvg = jnp.swapaxes(kvg, 2, 3)
                p_kv[g].append(kvg.astype(F32).reshape(B, keep, 2, ATT_HEADS, HEAD_DIM))
            qkvs = norm_qkv(xs, norm_mix[i], wqkv, cos_s[None], sin_s[None], None, tm=tms, out_dtype=F32)
            qkvs3 = jnp.swapaxes(qkvs.reshape(S, DB, QKV_COLS), 0, 1)
            nl = cache_kv_w128.shape[0]
            caches = [c.reshape(nl * DB, ATT_BLOCK, c.shape[2] // ATT_BLOCK, 2, ATT_HEADS, HEAD_DIM)
                      for c in (cache_kv_w128, cache_kv_w512, cache_kv_w2048)]
            os_ = attn_sample(qkvs3.reshape(DB, S, QKV_TILES, LANES), *caches, j, db=DB, s_len=S)
            xs = out_proj(jnp.swapaxes(os_.reshape(DB, S, HD_ALL), 0, 1).reshape(NS, HD_ALL), wo, zeros_d, xs, tm=tms)
            for g in range(N_GROUPS):
                kvg = qkvs3[:, :, g * GROUP_COLS + HD_ALL:(g + 1) * GROUP_COLS]
                s_kv[g].append(kvg.reshape(DB, S, 2, ATT_HEADS, HEAD_DIM))
        final = i == depth - 1
        w1 = ffn_w1[i].astype(BF16)
        w2 = ffn_w2[i].astype(BF16)
        xp = ffn(xp, norm_ffn[i], w1, w2, norm_final, tm=tmp, tf=512, final=final)
        xs = ffn(xs, norm_ffn[i], w1, w2, norm_final, tm=tms, tf=512, final=final)

    return (xp.reshape(B, T, D), jnp.swapaxes(xs.reshape(S, DB, D), 0, 1),
            jnp.stack(p_lru_conv), jnp.stack(p_lru_h), jnp.stack(p_cm_conv),
            jnp.stack(p_kv[0]), jnp.stack(p_kv[1]), jnp.stack(p_kv[2]),
            jnp.stack(s_lru_conv), jnp.stack(s_lru_h), jnp.stack(s_cm_conv),
            jnp.stack(s_kv[0]), jnp.stack(s_kv[1]), jnp.stack(s_kv[2]))
```

```python
import functools

import numpy as np
import jax
import jax.numpy as jnp
from jax import lax
from jax.experimental import pallas as pl
from jax.experimental.pallas import tpu as pltpu

F32 = jnp.float32
BF16 = jnp.bfloat16

LANES = 128
D_MODEL = 1024
D_RNN = 1408
LRU_BLOCKS = 16
LRU_BLOCK_DIM = D_RNN // LRU_BLOCKS
LRU_CONV_W = 4
LRU_C = 8.0
CM_KERNEL = 31
ATT_GROUPS = ((128, 1), (512, 4), (2048, 16))
N_GROUPS = 3
ATT_HEADS = 8
HEAD_DIM = 128
ATT_BLOCK = 128
ROPE_THETA = 10000.0
PAST_LEN = 2048
EPS = 1e-6
QKV_COLS = N_GROUPS * 3 * ATT_HEADS * HEAD_DIM
GROUP_COLS = 3 * ATT_HEADS * HEAD_DIM
HD_ALL = ATT_HEADS * HEAD_DIM
QKV_TILE = 512
N_CT = D_RNN // LANES
SCALE = HEAD_DIM ** -0.5
SCALE_LOG2E = SCALE * 1.4426950408889634
MIB = 1024 * 1024


def _cparams(sem, vmem_mib):
    return pltpu.CompilerParams(dimension_semantics=sem, vmem_limit_bytes=vmem_mib * MIB)


def _rms(x, g):
    ms = jnp.mean(x * x, axis=-1, keepdims=True)
    return (x * lax.rsqrt(ms + EPS)) * g


def _sigmoid(x):
    return 1.0 / (1.0 + jnp.exp(-x))


def _softplus(x):
    return jnp.maximum(x, 0.0) + jnp.log1p(jnp.exp(-jnp.abs(x)))


def _gelu_tanh(x):
    return x * (0.5 * (1.0 + jnp.tanh(0.7978845608028654 * (x + 0.044715 * (x * x * x)))))


def _norm_proj_gelu_kernel(x_ref, g_ref, w_ref, *rest, half, rc, conv_tiles):
    tm = x_ref.shape[0]
    if conv_tiles:
        cw_ref, cb_ref, o_ref, cs_ref, ubuf = rest
        i = pl.program_id(0)

        @pl.when(i % conv_tiles == 0)
        def _():
            ubuf[0:8, :] = jnp.zeros((8, half), F32)
    else:
        (o_ref,) = rest
    for r0 in range(0, tm, rc):
        xn = _rms(x_ref[r0:r0 + rc, :], g_ref[...]).astype(BF16)
        o_ref[r0:r0 + rc, 0:half] = _gelu_tanh(jnp.dot(xn, w_ref[:, 0:half], preferred_element_type=F32))
        u = jnp.dot(xn, w_ref[:, half:2 * half], preferred_element_type=F32)
        if not conv_tiles:
            o_ref[r0:r0 + rc, half:2 * half] = u
            continue
        ubuf[8 + r0:8 + r0 + rc, :] = u
        for c in range(half // LANES):
            cs = slice(c * LANES, (c + 1) * LANES)
            for q0 in range(r0, r0 + rc, 64):
                acc = cw_ref[0:1, cs] * ubuf[q0 + 5:q0 + 69, cs]
                for k in range(1, LRU_CONV_W):
                    acc = acc + cw_ref[k:k + 1, cs] * ubuf[q0 + 5 + k:q0 + 69 + k, cs]
                o_ref[q0:q0 + 64, half + c * LANES:half + (c + 1) * LANES] = acc + cb_ref[:, cs]
    if conv_tiles:
        @pl.when(i % conv_tiles == conv_tiles - 1)
        def _():
            cs_ref[0] = ubuf[tm + 5:tm + 8, :]

        ubuf[0:8, :] = ubuf[tm:tm + 8, :]


def norm_proj_gelu(x, g, w, conv=None, *, tm):
    n, d = x.shape
    nout = w.shape[1]
    half = nout // 2
    in_specs = [
        pl.BlockSpec((tm, d), lambda i: (i, 0)),
        pl.BlockSpec((1, d), lambda i: (0, 0)),
        pl.BlockSpec((d, nout), lambda i: (0, 0)),
    ]
    args = [x, g.reshape(1, d), w]
    out_specs = pl.BlockSpec((tm, nout), lambda i: (i, 0))
    out_shape = jax.ShapeDtypeStruct((n, nout), F32)
    scratch = []
    conv_tiles = 0
    if conv is not None:
        cw, cb, seq = conv
        conv_tiles = seq // tm
        in_specs += [pl.BlockSpec((LRU_CONV_W, half), lambda i: (0, 0)), pl.BlockSpec((1, half), lambda i: (0, 0))]
        args += [cw, cb.reshape(1, half)]
        out_specs = [out_specs, pl.BlockSpec((1, LRU_CONV_W - 1, half), lambda i: (i // conv_tiles, 0, 0))]
        out_shape = [out_shape, jax.ShapeDtypeStruct((n // seq, LRU_CONV_W - 1, half), F32)]
        scratch = [pltpu.VMEM((tm + 8, half), F32)]
    return pl.pallas_call(
        functools.partial(_norm_proj_gelu_kernel, half=half, rc=min(tm, 128), conv_tiles=conv_tiles),
        grid=(n // tm,),
        in_specs=in_specs,
        out_specs=out_specs,
        out_shape=out_shape,
        scratch_shapes=scratch,
        compiler_params=_cparams(("arbitrary",), 48),
        name="norm_proj_gelu",
    )(*args)


def _norm_glu_kernel(x_ref, g_ref, wa_ref, wg_ref, ba_ref, bg_ref, o_ref, xn_ref):
    @pl.when(pl.program_id(1) == 0)
    def _():
        xn_ref[...] = _rms(x_ref[...], g_ref[...]).astype(BF16)

    xn = xn_ref[...]
    a = jnp.dot(xn, wa_ref[...], preferred_element_type=F32) + ba_ref[...]
    gate = jnp.dot(xn, wg_ref[...], preferred_element_type=F32) + bg_ref[...]
    o_ref[...] = a * _sigmoid(gate)


def norm_glu(x, g, w, b, *, tm, tn):
    n, d = x.shape
    half = w.shape[1] // 2
    nb = half // tn
    b2 = b.reshape(1, 2 * half)
    return pl.pallas_call(
        _norm_glu_kernel,
        grid=(n // tm, nb),
        in_specs=[
            pl.BlockSpec((tm, d), lambda i, j: (i, 0)),
            pl.BlockSpec((1, d), lambda i, j: (0, 0)),
            pl.BlockSpec((d, tn), lambda i, j: (0, j)),
            pl.BlockSpec((d, tn), lambda i, j: (0, j + nb)),
            pl.BlockSpec((1, tn), lambda i, j: (0, j)),
            pl.BlockSpec((1, tn), lambda i, j: (0, j + nb)),
        ],
        out_specs=pl.BlockSpec((tm, tn), lambda i, j: (i, j)),
        out_shape=jax.ShapeDtypeStruct((n, half), F32),
        scratch_shapes=[pltpu.VMEM((tm, d), BF16)],
        compiler_params=_cparams(("parallel", "arbitrary"), 48),
        name="norm_glu",
    )(x, g.reshape(1, d), w, w, b2, b2)


def _norm_qkv_kernel(x_ref, g_ref, w_ref, cos_ref, sin_ref, *rest, permute):
    if permute:
        p_ref, o_ref, xn_ref = rest
    else:
        o_ref, xn_ref = rest
    j = pl.program_id(1)

    @pl.when(j == 0)
    def _():
        xn = _rms(x_ref[...], g_ref[...]).astype(BF16)
        xn_ref[0] = xn
        if permute:
            for g in range(1, N_GROUPS):
                xn_ref[g] = jnp.dot(p_ref[g - 1], xn, preferred_element_type=F32).astype(BF16)

    xn = xn_ref[j] if permute else xn_ref[0]
    cos = cos_ref[0]
    sin = sin_ref[0]
    for part in range(3):
        acc = jnp.dot(xn, w_ref[:, part * HD_ALL:(part + 1) * HD_ALL], preferred_element_type=F32)
        for h in range(ATT_HEADS):
            hs = slice(h * HEAD_DIM, (h + 1) * HEAD_DIM)
            xh = acc[:, hs]
            if part < 2:
                xh = xh * cos + pltpu.roll(xh, HEAD_DIM // 2, 1) * sin
            o_ref[:, part * HD_ALL + h * HEAD_DIM:part * HD_ALL + (h + 1) * HEAD_DIM] = xh.astype(o_ref.dtype)


def norm_qkv(x, g, w, cos, sin, perms, *, tm, out_dtype):
    n, d = x.shape
    period = cos.shape[1]
    nper = period // tm
    permute = perms is not None
    tab = (lambda i, j: (j, i % nper, 0)) if permute else (lambda i, j: (0, i % nper, 0))
    in_specs = [
        pl.BlockSpec((tm, d), lambda i, j: (i, 0)),
        pl.BlockSpec((1, d), lambda i, j: (0, 0)),
        pl.BlockSpec((d, GROUP_COLS), lambda i, j: (0, j)),
        pl.BlockSpec((1, tm, HEAD_DIM), tab),
        pl.BlockSpec((1, tm, HEAD_DIM), tab),
    ]
    args = [x, g.reshape(1, d), w, cos, sin]
    if permute:
        in_specs.append(pl.BlockSpec((N_GROUPS - 1, tm, tm), lambda i, j: (0, 0, 0)))
        args.append(perms)
    return pl.pallas_call(
        functools.partial(_norm_qkv_kernel, permute=permute),
        grid=(n // tm, N_GROUPS),
        in_specs=in_specs,
        out_specs=pl.BlockSpec((tm, GROUP_COLS), lambda i, j: (i, j)),
        out_shape=jax.ShapeDtypeStruct((n, QKV_COLS), out_dtype),
        scratch_shapes=[pltpu.VMEM((N_GROUPS if permute else 1, tm, d), BF16)],
        compiler_params=_cparams(("parallel", "arbitrary"), 48),
        name="norm_qkv",
    )(*args)


def _ffn_kernel(x_ref, g_ref, w1_ref, w2_ref, gf_ref, o_ref, xn_ref, acc_ref, *, final):
    f = pl.program_id(1)

    @pl.when(f == 0)
    def _():
        xn_ref[...] = _rms(x_ref[...], g_ref[...]).astype(BF16)
        acc_ref[...] = jnp.zeros_like(acc_ref)

    h = jnp.dot(xn_ref[...], w1_ref[...], preferred_element_type=F32)
    h = jnp.maximum(h, 0.0)
    h = (h * h).astype(BF16)
    acc_ref[...] += jnp.dot(h, w2_ref[...], preferred_element_type=F32)

    @pl.when(f == pl.num_programs(1) - 1)
    def _():
        y = x_ref[...] + acc_ref[...]
        if final:
            y = _rms(y, gf_ref[...])
        o_ref[...] = y


def ffn(x, g, w1, w2, gf, layer, *, tm, tf, final):
    n, d = x.shape
    dff = w1.shape[2]
    return pl.pallas_call(
        functools.partial(_ffn_kernel, final=final),
        grid=(n // tm, dff // tf),
        in_specs=[
            pl.BlockSpec((tm, d), lambda i, f: (i, 0)),
            pl.BlockSpec((1, d), lambda i, f: (0, 0)),
            pl.BlockSpec((None, d, tf), lambda i, f: (layer, 0, f)),
            pl.BlockSpec((None, tf, d), lambda i, f: (layer, f, 0)),
            pl.BlockSpec((1, d), lambda i, f: (0, 0)),
        ],
        out_specs=pl.BlockSpec((tm, d), lambda i, f: (i, 0)),
        out_shape=jax.ShapeDtypeStruct((n, d), F32),
        scratch_shapes=[pltpu.VMEM((tm, d), BF16), pltpu.VMEM((tm, d), F32)],
        compiler_params=_cparams(("parallel", "arbitrary"), 48),
        name="ffn",
    )(x, g.reshape(1, d), w1, w2, gf.reshape(1, d))


def _out_proj_kernel(a_ref, w_ref, b_ref, r_ref, o_ref):
    y = jnp.dot(a_ref[...].astype(BF16), w_ref[...], preferred_element_type=F32)
    o_ref[...] = r_ref[...] + (y + b_ref[...])


def out_proj(a, w, b, resid, *, tm):
    n, k = a.shape
    d = w.shape[1]
    return pl.pallas_call(
        _out_proj_kernel,
        grid=(n // tm,),
        in_specs=[
            pl.BlockSpec((tm, k), lambda i: (i, 0)),
            pl.BlockSpec((k, d), lambda i: (0, 0)),
            pl.BlockSpec((1, d), lambda i: (0, 0)),
            pl.BlockSpec((tm, d), lambda i: (i, 0)),
        ],
        out_specs=pl.BlockSpec((tm, d), lambda i: (i, 0)),
        out_shape=jax.ShapeDtypeStruct((n, d), F32),
        compiler_params=_cparams(("parallel",), 48),
        name="out_proj",
    )(a, w, b.reshape(1, d), resid)


def _unpermute_f32(pt, x):
    hi = x.astype(BF16)
    r1 = x - hi.astype(F32)
    mid = r1.astype(BF16)
    lo = (r1 - mid.astype(F32)).astype(BF16)
    out = jnp.dot(pt, hi, preferred_element_type=F32)
    out = out + jnp.dot(pt, mid, preferred_element_type=F32)
    return out + jnp.dot(pt, lo, preferred_element_type=F32)


def _merge_out_proj_kernel(o0_ref, o1_ref, o2_ref, l0_ref, l1_ref, l2_ref, pt_ref, w_ref, r_ref, o_ref, m_scr):
    outs = [o0_ref[...].astype(F32),
            jnp.dot(pt_ref[0], o1_ref[...], preferred_element_type=F32),
            jnp.dot(pt_ref[1], o2_ref[...], preferred_element_type=F32)]
    lses = [l0_ref[:, 0:ATT_HEADS],
            _unpermute_f32(pt_ref[0], l1_ref[...])[:, 0:ATT_HEADS],
            _unpermute_f32(pt_ref[1], l2_ref[...])[:, 0:ATT_HEADS]]
    mx = jnp.maximum(jnp.maximum(lses[0], lses[1]), lses[2])
    es = [jnp.exp(l - mx) for l in lses]
    inv = 1.0 / (es[0] + es[1] + es[2])
    ws = [e * inv for e in es]
    for h in range(ATT_HEADS):
        sl = slice(h * HEAD_DIM, (h + 1) * HEAD_DIM)
        acc = ws[0][:, h:h + 1] * outs[0][:, sl]
        acc = acc + ws[1][:, h:h + 1] * outs[1][:, sl]
        acc = acc + ws[2][:, h:h + 1] * outs[2][:, sl]
        m_scr[:, sl] = acc.astype(BF16)
    o_ref[...] = r_ref[...] + jnp.dot(m_scr[...], w_ref[...], preferred_element_type=F32)


def merge_out_proj(os_, ls_, perms_t, w, resid, *, tm):
    n = resid.shape[0]
    d = w.shape[1]
    ospec = pl.BlockSpec((tm, HD_ALL), lambda i: (i, 0))
    lspec = pl.BlockSpec((tm, LANES), lambda i: (i, 0))
    return pl.pallas_call(
        _merge_out_proj_kernel,
        grid=(n // tm,),
        in_specs=[ospec, ospec, ospec, lspec, lspec, lspec,
                  pl.BlockSpec((N_GROUPS - 1, tm, tm), lambda i: (0, 0, 0)),
                  pl.BlockSpec((HD_ALL, d), lambda i: (0, 0)),
                  pl.BlockSpec((tm, d), lambda i: (i, 0))],
        out_specs=pl.BlockSpec((tm, d), lambda i: (i, 0)),
        out_shape=jax.ShapeDtypeStruct((n, d), F32),
        scratch_shapes=[pltpu.VMEM((tm, HD_ALL), BF16)],
        compiler_params=_cparams(("parallel",), 48),
        name="merge_out_proj",
    )(*os_, *ls_, perms_t, w, resid)


def _band_slices(c):
    lo = max(c - 1, 0)
    hi = min(c + 2, N_CT)
    return (lo * LANES, hi * LANES), ((lo - (c - 1)) * LANES, (hi - (c - 1)) * LANES)


def _lru_ab(za, zx, u, sp):
    r = _sigmoid(za)
    i = _sigmoid(zx)
    log_a = (-LRU_C * r) * sp
    a = jnp.exp(log_a)
    b = jnp.sqrt(-jnp.tanh(log_a) * (a * a + 1.0)) * (i * u)
    return a, b


def _lru_gates(xg_tile_fn, wb_ref, c, u, gab, gxb, sp):
    (xlo, xhi), (rlo, rhi) = _band_slices(c)
    z = jnp.dot(xg_tile_fn(xlo, xhi), wb_ref[c, rlo:rhi, :], preferred_element_type=F32)
    return _lru_ab(z[:, :LANES] + gab, z[:, LANES:] + gxb, u, sp)


def _lru_mix_prompt_kernel(gu_ref, wb_ref, gab_ref, gxb_ref, lam_ref, hg_ref, hl_ref, hc_scr, *, tt):
    t = pl.program_id(1)
    nt = pl.num_programs(1)
    C = D_RNN
    SUB = 8

    @pl.when(t == 0)
    def _():
        hc_scr[...] = jnp.zeros((SUB, C), F32)

    sub = lax.broadcasted_iota(jnp.int32, (SUB, LANES), 0)
    keep = [sub >= d for d in (1, 2, 4)]
    sp = _softplus(-lam_ref[...])
    for c in range(N_CT):
        cs = slice(c * LANES, (c + 1) * LANES)
        ucs = slice(C + c * LANES, C + (c + 1) * LANES)
        (xlo, xhi), (rlo, rhi) = _band_slices(c)
        z = jnp.dot(gu_ref[:, C + xlo:C + xhi].astype(BF16), wb_ref[c, rlo:rhi, :], preferred_element_type=F32)
        hin = hc_scr[:, cs]
        for r0 in range(0, tt, 2 * SUB):
            hs = []
            for r1 in (r0, r0 + SUB):
                a, b = _lru_ab(z[r1:r1 + SUB, :LANES] + gab_ref[:, cs], z[r1:r1 + SUB, LANES:] + gxb_ref[:, cs],
                               gu_ref[r1:r1 + SUB, ucs], sp[:, cs])
                for d, kp in zip((1, 2, 4), keep):
                    a_sh = jnp.where(kp, pltpu.roll(a, d, 0), 1.0)
                    b_sh = jnp.where(kp, pltpu.roll(b, d, 0), 0.0)
                    b = a * b_sh + b
                    a = a * a_sh
                h = b + a * hin
                hin = jnp.broadcast_to(h[SUB - 1:SUB, :], (SUB, LANES))
                hs.append(h)
            h2 = jnp.concatenate(hs, axis=0)
            hg_ref[r0:r0 + 2 * SUB, cs] = (h2 * gu_ref[r0:r0 + 2 * SUB, cs]).astype(hg_ref.dtype)
        hc_scr[:, cs] = hin

    @pl.when(t == nt - 1)
    def _():
        hl_ref[0] = hc_scr[0:1, :]


def lru_mix_prompt(gu, wb, gab, gxb, lam, *, batch, seq, tt):
    C = D_RNN
    nt = seq // tt
    row = lambda v: v.reshape(1, C)
    full2 = lambda shape: pl.BlockSpec(shape, lambda b, t: (0,) * len(shape))
    return pl.pallas_call(
        functools.partial(_lru_mix_prompt_kernel, tt=tt),
        grid=(batch, nt),
        in_specs=[
            pl.BlockSpec((tt, 2 * C), lambda b, t: (b * nt + t, 0)),
            full2((N_CT, 3 * LANES, 2 * LANES)), full2((1, C)), full2((1, C)), full2((1, C)),
        ],
        out_specs=[
            pl.BlockSpec((tt, C), lambda b, t: (b * nt + t, 0)),
            pl.BlockSpec((1, 1, C), lambda b, t: (b, 0, 0)),
        ],
        out_shape=[
            jax.ShapeDtypeStruct((batch * seq, C), BF16),
            jax.ShapeDtypeStruct((batch, 1, C), F32),
        ],
        scratch_shapes=[pltpu.VMEM((8, C), F32)],
        compiler_params=_cparams(("arbitrary", "arbitrary"), 48),
        name="lru_mix_prompt",
    )(gu, wb, row(gab), row(gxb), row(lam))


def _lru_mix_sample_kernel(gu_ref, cst_ref, hp_ref, cw_ref, cb_ref, wb_ref, gab_ref, gxb_ref, lam_ref,
                           hg_ref, cso_ref, hl_ref, cv_scr, xg_scr, *, db, s_len):
    del db
    C = D_RNN
    W1 = LRU_CONV_W - 1

    def ext(j, cs):
        if j < W1:
            return cst_ref[j, :, cs]
        return gu_ref[j - W1, :, C + cs.start:C + cs.stop]

    for c in range(N_CT):
        cs = slice(c * LANES, (c + 1) * LANES)
        for s in range(s_len):
            acc = cw_ref[0:1, cs] * ext(s, cs)
            for k in range(1, LRU_CONV_W):
                acc = acc + cw_ref[k:k + 1, cs] * ext(s + k, cs)
            acc = acc + cb_ref[:, cs]
            cv_scr[s, :, cs] = acc
            xg_scr[s, :, cs] = acc.astype(BF16)
        for k in range(W1):
            cso_ref[k, :, cs] = ext(s_len + k, cs)

    sp = _softplus(-lam_ref[...])
    for c in range(N_CT):
        cs = slice(c * LANES, (c + 1) * LANES)
        h = hp_ref[:, cs]
        for s in range(s_len):
            a, b = _lru_gates(lambda lo, hi: xg_scr[s, :, lo:hi], wb_ref, c, cv_scr[s, :, cs],
                              gab_ref[:, cs], gxb_ref[:, cs], sp[:, cs])
            h = a * h + b
            hg_ref[s, :, cs] = h * gu_ref[s, :, cs]
        hl_ref[:, cs] = h


def lru_mix_sample(gu, cst, hp, cw, cb, wb, gab, gxb, lam, *, db, s_len):
    C = D_RNN
    row = lambda v: v.reshape(1, C)
    return pl.pallas_call(
        functools.partial(_lru_mix_sample_kernel, db=db, s_len=s_len),
        out_shape=[
            jax.ShapeDtypeStruct((s_len, db, C), F32),
            jax.ShapeDtypeStruct((LRU_CONV_W - 1, db, C), F32),
            jax.ShapeDtypeStruct((db, C), F32),
        ],
        scratch_shapes=[pltpu.VMEM((s_len, db, C), F32), pltpu.VMEM((s_len, db, C), BF16)],
        compiler_params=pltpu.CompilerParams(vmem_limit_bytes=48 * MIB),
        name="lru_mix_sample",
    )(gu, cst, hp, cw, row(cb), wb, row(gab), row(gxb), row(lam))


def _ln_silu(y, g, b):
    mu = jnp.mean(y, axis=-1, keepdims=True)
    yc = y - mu
    var = jnp.mean(yc * yc, axis=-1, keepdims=True)
    z = yc * lax.rsqrt(var + EPS) * g + b
    return z * _sigmoid(z)


def _cm_mix_prompt_kernel(u_ref, dw_ref, db_ref, lg_ref, lb_ref, o_ref, cs_ref, ubuf, cv_scr, *, tt):
    t = pl.program_id(1)
    nt = pl.num_programs(1)
    C = D_MODEL
    RB = 64
    HALO = 32
    OFF = HALO - (CM_KERNEL - 1)

    @pl.when(t == 0)
    def _():
        ubuf[0:HALO, :] = jnp.zeros((HALO, C), F32)

    ubuf[HALO:HALO + tt, :] = u_ref[...]

    WIN = RB + HALO
    for c in range(C // LANES):
        cs = slice(c * LANES, (c + 1) * LANES)
        for r0 in range(0, tt, RB):
            x = ubuf[r0:r0 + WIN, cs]
            acc = None
            for r in range(8):
                xr = x if r == 0 else pltpu.roll(x, WIN - r, 0)
                for o in range(OFF, OFF + CM_KERNEL):
                    if o % 8 != r:
                        continue
                    term = dw_ref[o - OFF:o - OFF + 1, cs] * xr[o - r:o - r + RB, :]
                    acc = term if acc is None else acc + term
            cv_scr[r0:r0 + RB, cs] = acc + db_ref[:, cs]

    @pl.when(t == nt - 1)
    def _():
        cs_ref[0] = ubuf[tt + OFF:tt + HALO, :]

    ubuf[0:HALO, :] = ubuf[tt:tt + HALO, :]

    for r0 in range(0, tt, RB):
        o_ref[r0:r0 + RB, :] = _ln_silu(cv_scr[r0:r0 + RB, :], lg_ref[...], lb_ref[...]).astype(o_ref.dtype)


def cm_mix_prompt(u, dw, db_, lg, lb, *, batch, seq, tt):
    C = D_MODEL
    nt = seq // tt
    row = lambda v: v.reshape(1, C)
    full2 = lambda shape: pl.BlockSpec(shape, lambda b, t: (0,) * len(shape))
    return pl.pallas_call(
        functools.partial(_cm_mix_prompt_kernel, tt=tt),
        grid=(batch, nt),
        in_specs=[
            pl.BlockSpec((tt, C), lambda b, t: (b * nt + t, 0)),
            full2((CM_KERNEL, C)), full2((1, C)), full2((1, C)), full2((1, C)),
        ],
        out_specs=[
            pl.BlockSpec((tt, C), lambda b, t: (b * nt + t, 0)),
            pl.BlockSpec((1, CM_KERNEL - 1, C), lambda b, t: (b, 0, 0)),
        ],
        out_shape=[
            jax.ShapeDtypeStruct((batch * seq, C), BF16),
            jax.ShapeDtypeStruct((batch, CM_KERNEL - 1, C), F32),
        ],
        scratch_shapes=[pltpu.VMEM((tt + 32, C), F32), pltpu.VMEM((tt, C), F32)],
        compiler_params=_cparams(("arbitrary", "arbitrary"), 48),
        name="cm_mix_prompt",
    )(u, dw, row(db_), row(lg), row(lb))


def _cm_mix_sample_kernel(u_ref, st_ref, dw_ref, db_ref, lg_ref, lb_ref, o_ref, so_ref, cv_scr, *, nb, s_len):
    del nb
    C = D_MODEL
    W1 = CM_KERNEL - 1

    def ext(j, cs):
        if j < W1:
            return st_ref[:, j, cs]
        return u_ref[j - W1, :, cs]

    for c in range(C // LANES):
        cs = slice(c * LANES, (c + 1) * LANES)
        for s in range(s_len):
            acc = dw_ref[0:1, cs] * ext(s, cs)
            for k in range(1, CM_KERNEL):
                acc = acc + dw_ref[k:k + 1, cs] * ext(s + k, cs)
            cv_scr[s, :, cs] = acc + db_ref[:, cs]
        for j in range(W1):
            so_ref[:, j, cs] = ext(j + s_len, cs)

    for s in range(s_len):
        o_ref[s] = _ln_silu(cv_scr[s], lg_ref[...], lb_ref[...])


def cm_mix_sample(u, st, dw, db_, lg, lb, *, db, s_len, nb):
    C = D_MODEL
    W1 = CM_KERNEL - 1
    row = lambda v: v.reshape(1, C)
    full1 = lambda shape: pl.BlockSpec(shape, lambda i: (0,) * len(shape))
    return pl.pallas_call(
        functools.partial(_cm_mix_sample_kernel, nb=nb, s_len=s_len),
        grid=(db // nb,),
        in_specs=[
            pl.BlockSpec((s_len, nb, C), lambda i: (0, i, 0)),
            pl.BlockSpec((nb, W1, C), lambda i: (i, 0, 0)),
            full1((CM_KERNEL, C)), full1((1, C)), full1((1, C)), full1((1, C)),
        ],
        out_specs=[
            pl.BlockSpec((s_len, nb, C), lambda i: (0, i, 0)),
            pl.BlockSpec((nb, W1, C), lambda i: (i, 0, 0)),
        ],
        out_shape=[
            jax.ShapeDtypeStruct((s_len, db, C), F32),
            jax.ShapeDtypeStruct((db, W1, C), F32),
        ],
        scratch_shapes=[pltpu.VMEM((s_len, nb, C), F32)],
        compiler_params=_cparams(("parallel",), 48),
        name="cm_mix_sample",
    )(u, st, dw, row(db_), row(lg), row(lb))


def _attn_prompt_kernel(q_ref, k_ref, v_ref, kp_ref, vp_ref, o_ref, l_ref, qs, kext, vext, os, ls, s_scr, *, ch):
    n = pl.program_id(2)
    qs[...] = q_ref[:, 0].reshape(ch, HD_ALL)
    kext[0:ATT_BLOCK, :] = kp_ref[:, 0].reshape(ATT_BLOCK, HD_ALL)
    kext[ATT_BLOCK:ATT_BLOCK + ch, :] = k_ref[:, 0].reshape(ch, HD_ALL)
    vext[0:ATT_BLOCK, :] = vp_ref[:, 0].reshape(ATT_BLOCK, HD_ALL)
    vext[ATT_BLOCK:ATT_BLOCK + ch, :] = v_ref[:, 0].reshape(ch, HD_ALL)
    qi = lax.broadcasted_iota(jnp.int32, (ATT_BLOCK, 2 * ATT_BLOCK), 0)
    kj = lax.broadcasted_iota(jnp.int32, (ATT_BLOCK, 2 * ATT_BLOCK), 1)
    band = jnp.logical_and(kj >= qi, kj <= qi + ATT_BLOCK)
    band_first = jnp.logical_and(band, jnp.logical_or(kj >= ATT_BLOCK, n > 0))
    lane = lax.broadcasted_iota(jnp.int32, (ATT_BLOCK, LANES), 1)
    dn = (((1,), (1,)), ((), ()))
    for i in range(ch // ATT_BLOCK):
        rows = slice(i * ATT_BLOCK, (i + 1) * ATT_BLOCK)
        krows = slice(i * ATT_BLOCK, (i + 2) * ATT_BLOCK)
        mask = band_first if i == 0 else band
        for h in range(ATT_HEADS):
            hs = slice(h * HEAD_DIM, (h + 1) * HEAD_DIM)
            s_scr[i % 2, h] = lax.dot_general(qs[rows, hs], kext[krows, hs], dn, preferred_element_type=F32)
        lse_tile = jnp.zeros((ATT_BLOCK, LANES), F32)
        for h in range(ATT_HEADS):
            hs = slice(h * HEAD_DIM, (h + 1) * HEAD_DIM)
            s = jnp.where(mask, s_scr[i % 2, h], -jnp.inf)
            m = jnp.max(jnp.maximum(s[:, :ATT_BLOCK], s[:, ATT_BLOCK:]), axis=-1, keepdims=True)
            p = jnp.exp2((s - m) * SCALE_LOG2E)
            l = jnp.sum(p[:, :ATT_BLOCK] + p[:, ATT_BLOCK:], axis=-1, keepdims=True)
            o = jnp.dot(p.astype(BF16), vext[krows, hs], preferred_element_type=F32)
            os[rows, hs] = (o * (1.0 / l)).astype(os.dtype)
            lse_tile = jnp.where(lane == h, m * SCALE + jnp.log(l), lse_tile)
        ls[rows, :] = lse_tile
    o_ref[:, 0] = os[...].reshape(o_ref.shape[0], o_ref.shape[2], HD_ALL)
    l_ref[:, 0] = ls[...].reshape(l_ref.shape[0], l_ref.shape[2], LANES)


def attn_prompt_group(qkv, g, *, batch, seq):
    win, dil = ATT_GROUPS[g]
    m = seq // dil
    ch = min(m, 512)
    nch = m // ch
    bpc = ch // ATT_BLOCK
    nblk = m // ATT_BLOCK
    rpt = QKV_TILE // dil
    ntc = ch // rpt
    prpt = min(rpt, ATT_BLOCK)
    ntiles = batch * seq // QKV_TILE
    cur_view = qkv.reshape(ntiles, dil, rpt, QKV_COLS)
    prev_view = qkv.reshape(ntiles * (rpt // prpt), dil, prpt, QKV_COLS)
    ncb = QKV_COLS // HD_ALL
    cq, ck, cv = 3 * g, 3 * g + 1, 3 * g + 2

    def cur(cb):
        return pl.BlockSpec((ntc, 1, rpt, HD_ALL), lambda b, r, n: (b * nch + n, r, 0, cb))

    def prev(cb):
        return pl.BlockSpec((ATT_BLOCK // prpt, 1, prpt, HD_ALL),
                            lambda b, r, n: (b * nblk + jnp.maximum(n * bpc - 1, 0), r, 0, cb))

    o, l = pl.pallas_call(
        functools.partial(_attn_prompt_kernel, ch=ch),
        grid=(batch, dil, nch),
        in_specs=[cur(cq), cur(ck), cur(cv), prev(ck), prev(cv)],
        out_specs=[pl.BlockSpec((ntc, 1, rpt, HD_ALL), lambda b, r, n: (b * nch + n, r, 0, 0)),
                   pl.BlockSpec((ntc, 1, rpt, LANES), lambda b, r, n: (b * nch + n, r, 0, 0))],
        out_shape=[jax.ShapeDtypeStruct((ntiles, dil, rpt, HD_ALL), BF16),
                   jax.ShapeDtypeStruct((ntiles, dil, rpt, LANES), F32)],
        scratch_shapes=[pltpu.VMEM((ch, HD_ALL), BF16),
                        pltpu.VMEM((ATT_BLOCK + ch, HD_ALL), BF16), pltpu.VMEM((ATT_BLOCK + ch, HD_ALL), BF16),
                        pltpu.VMEM((ch, HD_ALL), BF16), pltpu.VMEM((ch, LANES), F32),
                        pltpu.VMEM((2, ATT_HEADS, ATT_BLOCK, 2 * ATT_BLOCK), F32)],
        compiler_params=_cparams(("parallel", "parallel", "arbitrary"), 48),
        name=f"attn_prompt_g{g}",
    )(cur_view, cur_view, cur_view, prev_view, prev_view)
    return o.reshape(batch * seq, HD_ALL), l.reshape(batch * seq, LANES)


QKV_TILES = QKV_COLS // LANES
GROUP_TILES = GROUP_COLS // LANES


def _attn_sample_kernel(qkv_ref, c0_ref, c1_ref, c2_ref, o_ref, s_scr, p_scr, *, s_len):
    nk = ATT_BLOCK * ATT_HEADS
    caches = (c0_ref, c1_ref, c2_ref)
    col = lax.broadcasted_iota(jnp.int32, (ATT_HEADS, nk), 1)
    rowh = lax.broadcasted_iota(jnp.int32, (ATT_HEADS, nk), 0)
    own = (col & (ATT_HEADS - 1)) == rowh
    key = col >> 3
    dn = (((1,), (1,)), ((), ()))

    def kvflat(g, s, which):
        r = 0 if g == 0 else s
        return caches[g][0, :, r, which].reshape(nk, HEAD_DIM).astype(BF16)

    def tile(s, g, which):
        lo = g * GROUP_TILES + which * ATT_HEADS
        return qkv_ref[0, s, lo:lo + ATT_HEADS, :]

    def rnd(x):
        return x.astype(BF16).astype(F32)

    q0 = qkv_ref[0, :, 0:ATT_HEADS, :].reshape(s_len * ATT_HEADS, HEAD_DIM).astype(BF16)
    s0 = lax.dot_general(q0, kvflat(0, 0, 0), dn, preferred_element_type=F32)
    s_scr[0:s_len] = s0.reshape(s_len, ATT_HEADS, nk)
    for g in range(1, N_GROUPS):
        for s in range(s_len):
            s_scr[g * s_len + s] = lax.dot_general(tile(s, g, 0).astype(BF16), kvflat(g, s, 0), dn,
                                                   preferred_element_type=F32)

    stats = {}
    for g in range(N_GROUPS):
        for s in range(s_len):
            idx = g * s_len + s
            valid = jnp.logical_and(own, key >= s) if g == 0 else own
            sc = jnp.where(valid, s_scr[idx], -jnp.inf)
            news = tuple(range(s + 1)) if g == 0 else (s,)
            qf = rnd(tile(s, g, 0))
            sn = [jnp.sum(qf * rnd(tile(j, g, 1)), axis=-1, keepdims=True) for j in news]
            m = jnp.max(sc, axis=-1, keepdims=True)
            for x in sn:
                m = jnp.maximum(m, x)
            p = jnp.exp2((sc - m) * SCALE_LOG2E)
            pn = [jnp.exp2((x - m) * SCALE_LOG2E) for x in sn]
            l = jnp.sum(p, axis=-1, keepdims=True)
            for x in pn:
                l = l + x
            p_scr[idx] = p
            stats[(g, s)] = (m, l, pn, news)

    outs = {}
    p0 = p_scr[0:s_len].reshape(s_len * ATT_HEADS, nk).astype(BF16)
    o0 = jnp.dot(p0, kvflat(0, 0, 1), preferred_element_type=F32).reshape(s_len, ATT_HEADS, HEAD_DIM)
    for g in range(N_GROUPS):
        for s in range(s_len):
            idx = g * s_len + s
            m, l, pn, news = stats[(g, s)]
            if g == 0:
                o = o0[s]
            else:
                o = jnp.dot(p_scr[idx].astype(BF16), kvflat(g, s, 1), preferred_element_type=F32)
            for x, j in zip(pn, news):
                o = o + rnd(x) * rnd(tile(j, g, 2))
            outs[(g, s)] = (o * (1.0 / l), m * SCALE + jnp.log(l))

    for s in range(s_len):
        lses = [outs[(g, s)][1] for g in range(N_GROUPS)]
        mx = jnp.maximum(jnp.maximum(lses[0], lses[1]), lses[2])
        es = [jnp.exp(x - mx) for x in lses]
        inv = 1.0 / (es[0] + es[1] + es[2])
        o_ref[0, s] = ((es[0] * inv) * outs[(0, s)][0] + (es[1] * inv) * outs[(1, s)][0]
                       + (es[2] * inv) * outs[(2, s)][0])


def attn_sample(qkv, c0, c1, c2, layer, *, db, s_len):
    nk = ATT_BLOCK * ATT_HEADS

    def cspec(c):
        r = min(c.shape[2], s_len)
        return pl.BlockSpec((1, ATT_BLOCK, r, 2, ATT_HEADS, HEAD_DIM), lambda b: (layer * db + b, 0, 0, 0, 0, 0))

    return pl.pallas_call(
        functools.partial(_attn_sample_kernel, s_len=s_len),
        grid=(db,),
        in_specs=[pl.BlockSpec((1, s_len, QKV_TILES, LANES), lambda b: (b, 0, 0, 0)),
                  cspec(c0), cspec(c1), cspec(c2)],
        out_specs=pl.BlockSpec((1, s_len, ATT_HEADS, HEAD_DIM), lambda b: (b, 0, 0, 0)),
        out_shape=jax.ShapeDtypeStruct((db, s_len, ATT_HEADS, HEAD_DIM), F32),
        scratch_shapes=[pltpu.VMEM((N_GROUPS * s_len, ATT_HEADS, nk), F32),
                        pltpu.VMEM((N_GROUPS * s_len, ATT_HEADS, nk), F32)],
        compiler_params=_cparams(("arbitrary",), 56),
        name="attn_sample",
    )(qkv, c0, c1, c2)


def _gate_band(ga_w, gx_w):
    eye = jnp.eye(LRU_BLOCKS, dtype=F32)

    def dense(w):
        d = jnp.einsum('ncd,nm->ncmd', w, eye).reshape(D_RNN, D_RNN)
        return jnp.pad(d, ((LANES, LANES), (0, 0)))

    da, dx = dense(ga_w), dense(gx_w)
    bands = [jnp.concatenate([da[c * LANES:(c + 3) * LANES, c * LANES:(c + 1) * LANES],
                              dx[c * LANES:(c + 3) * LANES, c * LANES:(c + 1) * LANES]], axis=1)
             for c in range(N_CT)]
    return jnp.stack(bands).astype(BF16)


def _tile_order(dil):
    rows = np.arange(QKV_TILE)
    rpt = QKV_TILE // dil
    return (rows % rpt) * dil + rows // rpt


def _perm_matrices():
    mats = []
    for _, dil in ATT_GROUPS[1:]:
        p = np.zeros((QKV_TILE, QKV_TILE), np.float32)
        p[np.arange(QKV_TILE), _tile_order(dil)] = 1.0
        mats.append(p)
    return np.stack(mats)


def _rope_tables(pos):
    half = HEAD_DIM // 2
    inv_freq = ROPE_THETA ** (-jnp.arange(half, dtype=F32) / half)
    ang = pos.astype(F32)[:, None] * inv_freq
    cos, sin = jnp.cos(ang), jnp.sin(ang)
    return jnp.concatenate([cos, cos], axis=-1), jnp.concatenate([-sin, sin], axis=-1)


def _row_tile(n, cap):
    t = cap
    while n % t:
        t //= 2
    return t


def kernel(x_prompt, x_sample, state_lru_conv, state_lru_h, state_cm_conv, cache_kv_w128, cache_kv_w512, cache_kv_w2048, norm_mix, norm_ffn, norm_final, lru_w_in, lru_conv_w, lru_conv_b, lru_gate_a_w, lru_gate_a_b, lru_gate_x_w, lru_gate_x_b, lru_lambda, lru_w_out, cm_w_pw1, cm_b_pw1, cm_dw_w, cm_dw_b, cm_ln_g, cm_ln_b, cm_w_pw2, cm_b_pw2, att_w_qkv, att_w_o, ffn_w1, ffn_w2):
    B, T, D = x_prompt.shape
    DB, S, _ = x_sample.shape
    depth = norm_mix.shape[0]
    NP, NS = B * T, DB * S
    assert D == D_MODEL and T % (ATT_GROUPS[-1][1] * ATT_BLOCK) == 0 and T % QKV_TILE == 0 and S == 4
    for c, (win, _) in zip((cache_kv_w128, cache_kv_w512, cache_kv_w2048), ATT_GROUPS):
        assert c.shape[2] == win
    tmp = _row_tile(NP, 1024)
    tms = _row_tile(NS, 512)
    zeros_d = jnp.zeros((D,), F32)

    xp = x_prompt.reshape(NP, D)
    xs = jnp.swapaxes(x_sample, 0, 1).reshape(NS, D)

    pos_p = np.arange(T).reshape(T // QKV_TILE, QKV_TILE)
    pos_g = np.stack([pos_p[:, _tile_order(dil)].reshape(T) for _, dil in ATT_GROUPS])
    cos_p, sin_p = _rope_tables(jnp.asarray(pos_g.reshape(-1)))
    cos_p, sin_p = cos_p.reshape(N_GROUPS, T, HEAD_DIM), sin_p.reshape(N_GROUPS, T, HEAD_DIM)
    pm = _perm_matrices()
    perms = jnp.asarray(pm, BF16)
    perms_t = jnp.asarray(np.swapaxes(pm, 1, 2), BF16)
    cos_s, sin_s = _rope_tables(PAST_LEN + jnp.arange(NS) // DB)

    ffn_w1b, ffn_w2b = ffn_w1.astype(BF16), ffn_w2.astype(BF16)

    p_lru_conv, p_lru_h, s_lru_conv, s_lru_h = [], [], [], []
    p_cm_conv, s_cm_conv = [], []
    p_kv = [[] for _ in ATT_GROUPS]
    s_kv = [[] for _ in ATT_GROUPS]

    for i in range(depth):
        m, j = i % 3, i // 3
        if m == 0:
            w_in = lru_w_in[j].astype(BF16)
            w_out = lru_w_out[j].astype(BF16)
            wb = _gate_band(lru_gate_a_w[j], lru_gate_x_w[j])
            lp = (lru_conv_w[j], lru_conv_b[j], wb, lru_gate_a_b[j], lru_gate_x_b[j], lru_lambda[j])
            gu, cp = norm_proj_gelu(xp, norm_mix[i], w_in, (lru_conv_w[j], lru_conv_b[j], T), tm=_row_tile(T, 512))
            hg, hp = lru_mix_prompt(gu, wb, lru_gate_a_b[j], lru_gate_x_b[j], lru_lambda[j], batch=B, seq=T, tt=512)
            xp = out_proj(hg, w_out, zeros_d, xp, tm=tmp)
            p_lru_conv.append(cp)
            p_lru_h.append(hp.reshape(B, D_RNN))
            gus = norm_proj_gelu(xs, norm_mix[i], w_in, tm=tms)
            hgs, cs, hs = lru_mix_sample(gus.reshape(S, DB, 2 * D_RNN), jnp.swapaxes(state_lru_conv[j], 0, 1),
                                         state_lru_h[j], *lp, db=DB, s_len=S)
            xs = out_proj(hgs.reshape(NS, D_RNN), w_out, zeros_d, xs, tm=tms)
            s_lru_conv.append(jnp.swapaxes(cs, 0, 1))
            s_lru_h.append(hs)
        elif m == 1:
            w1 = cm_w_pw1[j].astype(BF16)
            w2 = cm_w_pw2[j].astype(BF16)
            cp_ = (cm_dw_w[j], cm_dw_b[j], cm_ln_g[j], cm_ln_b[j])
            u = norm_glu(xp, norm_mix[i], w1, cm_b_pw1[j], tm=tmp, tn=512)
            y, cp = cm_mix_prompt(u, *cp_, batch=B, seq=T, tt=512)
            xp = out_proj(y, w2, cm_b_pw2[j], xp, tm=tmp)
            p_cm_conv.append(cp)
            us = norm_glu(xs, norm_mix[i], w1, cm_b_pw1[j], tm=tms, tn=512)
            nb = _row_tile(DB, 32)
            ys, cs = cm_mix_sample(us.reshape(S, DB, D), state_cm_conv[j], *cp_,
                                   db=DB, s_len=S, nb=nb)
            xs = out_proj(ys.reshape(NS, D), w2, cm_b_pw2[j], xs, tm=tms)
            s_cm_conv.append(cs)
        else:
            wqkv = att_w_qkv[j].astype(BF16)
            wo = att_w_o[j].astype(BF16)
            qkv = norm_qkv(xp, norm_mix[i], wqkv, cos_p, sin_p, perms, tm=QKV_TILE, out_dtype=BF16)
            og = [attn_prompt_group(qkv, g, batch=B, seq=T) for g in range(N_GROUPS)]
            xp = merge_out_proj([o for o, _ in og], [l for _, l in og], perms_t, wo, xp, tm=QKV_TILE)
            for g, (win, dil) in enumerate(ATT_GROUPS):
                keep = min(win, T)
                kvg = qkv.reshape(B, T, QKV_COLS)[:, T - keep:, g * GROUP_COLS + HD_ALL:(g + 1) * GROUP_COLS]
                if dil > 1:
                    kvg = kvg.reshape(B, keep // QKV_TILE, dil, QKV_TILE // dil, 2 * HD_ALL)
                    kvg = jnp.swapaxes(kvg, 2, 3)
                p_kv[g].append(kvg.astype(F32).reshape(B, keep, 2, ATT_HEADS, HEAD_DIM))
            qkvs = norm_qkv(xs, norm_mix[i], wqkv, cos_s[None], sin_s[None], None, tm=tms, out_dtype=F32)
            qkvs3 = jnp.swapaxes(qkvs.reshape(S, DB, QKV_COLS), 0, 1)
            nl = cache_kv_w128.shape[0]
            caches = [c.reshape(nl * DB, ATT_BLOCK, c.shape[2] // ATT_BLOCK, 2, ATT_HEADS, HEAD_DIM)
                      for c in (cache_kv_w128, cache_kv_w512, cache_kv_w2048)]
            os_ = attn_sample(qkvs3.reshape(DB, S, QKV_TILES, LANES), *caches, j, db=DB, s_len=S)
            xs = out_proj(jnp.swapaxes(os_.reshape(DB, S, HD_ALL), 0, 1).reshape(NS, HD_ALL), wo, zeros_d, xs, tm=tms)
            for g in range(N_GROUPS):
                kvg = qkvs3[:, :, g * GROUP_COLS + HD_ALL:(g + 1) * GROUP_COLS]
                s_kv[g].append(kvg.reshape(DB, S, 2, ATT_HEADS, HEAD_DIM))
        final = i == depth - 1
        xp = ffn(xp, norm_ffn[i], ffn_w1b, ffn_w2b, norm_final, i, tm=tmp, tf=512, final=final)
        xs = ffn(xs, norm_ffn[i], ffn_w1b, ffn_w2b, norm_final, i, tm=tms, tf=512, final=final)

    return (xp.reshape(B, T, D), jnp.swapaxes(xs.reshape(S, DB, D), 0, 1),
            jnp.stack(p_lru_conv), jnp.stack(p_lru_h), jnp.stack(p_cm_conv),
            jnp.stack(p_kv[0]), jnp.stack(p_kv[1]), jnp.stack(p_kv[2]),
            jnp.stack(s_lru_conv), jnp.stack(s_lru_h), jnp.stack(s_cm_conv),
            jnp.stack(s_kv[0]), jnp.stack(s_kv[1]), jnp.stack(s_kv[2]))
```

```python
import functools

import numpy as np
import jax
import jax.numpy as jnp
from jax import lax
from jax.experimental import pallas as pl
from jax.experimental.pallas import tpu as pltpu

F32 = jnp.float32
BF16 = jnp.bfloat16

LANES = 128
D_MODEL = 1024
D_RNN = 1408
LRU_BLOCKS = 16
LRU_BLOCK_DIM = D_RNN // LRU_BLOCKS
LRU_CONV_W = 4
LRU_C = 8.0
CM_KERNEL = 31
ATT_GROUPS = ((128, 1), (512, 4), (2048, 16))
N_GROUPS = 3
ATT_HEADS = 8
HEAD_DIM = 128
ATT_BLOCK = 128
ROPE_THETA = 10000.0
PAST_LEN = 2048
EPS = 1e-6
QKV_COLS = N_GROUPS * 3 * ATT_HEADS * HEAD_DIM
GROUP_COLS = 3 * ATT_HEADS * HEAD_DIM
HD_ALL = ATT_HEADS * HEAD_DIM
QKV_TILE = 512
N_CT = D_RNN // LANES
SCALE = HEAD_DIM ** -0.5
SCALE_LOG2E = SCALE * 1.4426950408889634
MIB = 1024 * 1024


def _cparams(sem, vmem_mib):
    return pltpu.CompilerParams(dimension_semantics=sem, vmem_limit_bytes=vmem_mib * MIB)


def _rms(x, g):
    ms = jnp.mean(x * x, axis=-1, keepdims=True)
    return (x * lax.rsqrt(ms + EPS)) * g


def _sigmoid(x):
    return 1.0 / (1.0 + jnp.exp(-x))


def _softplus(x):
    return jnp.maximum(x, 0.0) + jnp.log1p(jnp.exp(-jnp.abs(x)))


def _gelu_tanh(x):
    return x * (0.5 * (1.0 + jnp.tanh(0.7978845608028654 * (x + 0.044715 * (x * x * x)))))


def _norm_proj_gelu_kernel(x_ref, g_ref, w_ref, *rest, half, rc, conv_tiles):
    tm = x_ref.shape[0]
    if conv_tiles:
        cw_ref, cb_ref, o_ref, cs_ref, ubuf = rest
        i = pl.program_id(0)

        @pl.when(i % conv_tiles == 0)
        def _():
            ubuf[0:8, :] = jnp.zeros((8, half), F32)
    else:
        (o_ref,) = rest
    for r0 in range(0, tm, rc):
        xn = _rms(x_ref[r0:r0 + rc, :], g_ref[...]).astype(BF16)
        o_ref[r0:r0 + rc, 0:half] = _gelu_tanh(jnp.dot(xn, w_ref[:, 0:half], preferred_element_type=F32))
        u = jnp.dot(xn, w_ref[:, half:2 * half], preferred_element_type=F32)
        if not conv_tiles:
            o_ref[r0:r0 + rc, half:2 * half] = u
            continue
        ubuf[8 + r0:8 + r0 + rc, :] = u
        for c in range(half // LANES):
            cs = slice(c * LANES, (c + 1) * LANES)
            for q0 in range(r0, r0 + rc, 64):
                acc = cw_ref[0:1, cs] * ubuf[q0 + 5:q0 + 69, cs]
                for k in range(1, LRU_CONV_W):
                    acc = acc + cw_ref[k:k + 1, cs] * ubuf[q0 + 5 + k:q0 + 69 + k, cs]
                o_ref[q0:q0 + 64, half + c * LANES:half + (c + 1) * LANES] = acc + cb_ref[:, cs]
    if conv_tiles:
        @pl.when(i % conv_tiles == conv_tiles - 1)
        def _():
            cs_ref[0] = ubuf[tm + 5:tm + 8, :]

        ubuf[0:8, :] = ubuf[tm:tm + 8, :]


def norm_proj_gelu(x, g, w, conv=None, *, tm):
    n, d = x.shape
    nout = w.shape[1]
    half = nout // 2
    in_specs = [
        pl.BlockSpec((tm, d), lambda i: (i, 0)),
        pl.BlockSpec((1, d), lambda i: (0, 0)),
        pl.BlockSpec((d, nout), lambda i: (0, 0)),
    ]
    args = [x, g.reshape(1, d), w]
    out_specs = pl.BlockSpec((tm, nout), lambda i: (i, 0))
    out_shape = jax.ShapeDtypeStruct((n, nout), F32)
    scratch = []
    conv_tiles = 0
    if conv is not None:
        cw, cb, seq = conv
        conv_tiles = seq // tm
        in_specs += [pl.BlockSpec((LRU_CONV_W, half), lambda i: (0, 0)), pl.BlockSpec((1, half), lambda i: (0, 0))]
        args += [cw, cb.reshape(1, half)]
        out_specs = [out_specs, pl.BlockSpec((1, LRU_CONV_W - 1, half), lambda i: (i // conv_tiles, 0, 0))]
        out_shape = [out_shape, jax.ShapeDtypeStruct((n // seq, LRU_CONV_W - 1, half), F32)]
        scratch = [pltpu.VMEM((tm + 8, half), F32)]
    return pl.pallas_call(
        functools.partial(_norm_proj_gelu_kernel, half=half, rc=min(tm, 128), conv_tiles=conv_tiles),
        grid=(n // tm,),
        in_specs=in_specs,
        out_specs=out_specs,
        out_shape=out_shape,
        scratch_shapes=scratch,
        compiler_params=_cparams(("arbitrary",), 48),
        name="norm_proj_gelu",
    )(*args)


def _norm_glu_kernel(x_ref, g_ref, wa_ref, wg_ref, ba_ref, bg_ref, o_ref, xn_ref):
    @pl.when(pl.program_id(1) == 0)
    def _():
        xn_ref[...] = _rms(x_ref[...], g_ref[...]).astype(BF16)

    xn = xn_ref[...]
    a = jnp.dot(xn, wa_ref[...], preferred_element_type=F32) + ba_ref[...]
    gate = jnp.dot(xn, wg_ref[...], preferred_element_type=F32) + bg_ref[...]
    o_ref[...] = a * _sigmoid(gate)


def norm_glu(x, g, w, b, *, tm, tn):
    n, d = x.shape
    half = w.shape[1] // 2
    nb = half // tn
    b2 = b.reshape(1, 2 * half)
    return pl.pallas_call(
        _norm_glu_kernel,
        grid=(n // tm, nb),
        in_specs=[
            pl.BlockSpec((tm, d), lambda i, j: (i, 0)),
            pl.BlockSpec((1, d), lambda i, j: (0, 0)),
            pl.BlockSpec((d, tn), lambda i, j: (0, j)),
            pl.BlockSpec((d, tn), lambda i, j: (0, j + nb)),
            pl.BlockSpec((1, tn), lambda i, j: (0, j)),
            pl.BlockSpec((1, tn), lambda i, j: (0, j + nb)),
        ],
        out_specs=pl.BlockSpec((tm, tn), lambda i, j: (i, j)),
        out_shape=jax.ShapeDtypeStruct((n, half), F32),
        scratch_shapes=[pltpu.VMEM((tm, d), BF16)],
        compiler_params=_cparams(("parallel", "arbitrary"), 48),
        name="norm_glu",
    )(x, g.reshape(1, d), w, w, b2, b2)


def _norm_qkv_kernel(x_ref, g_ref, w_ref, cos_ref, sin_ref, *rest, permute):
    if permute:
        p_ref, o_ref, xn_ref = rest
    else:
        o_ref, xn_ref = rest
    j = pl.program_id(1)

    @pl.when(j == 0)
    def _():
        xn = _rms(x_ref[...], g_ref[...]).astype(BF16)
        xn_ref[0] = xn
        if permute:
            for g in range(1, N_GROUPS):
                xn_ref[g] = jnp.dot(p_ref[g - 1], xn, preferred_element_type=F32).astype(BF16)

    xn = xn_ref[j] if permute else xn_ref[0]
    cos = cos_ref[0]
    sin = sin_ref[0]
    for part in range(3):
        acc = jnp.dot(xn, w_ref[:, part * HD_ALL:(part + 1) * HD_ALL], preferred_element_type=F32)
        for h in range(ATT_HEADS):
            hs = slice(h * HEAD_DIM, (h + 1) * HEAD_DIM)
            xh = acc[:, hs]
            if part < 2:
                xh = xh * cos + pltpu.roll(xh, HEAD_DIM // 2, 1) * sin
            o_ref[:, part * HD_ALL + h * HEAD_DIM:part * HD_ALL + (h + 1) * HEAD_DIM] = xh.astype(o_ref.dtype)


def norm_qkv(x, g, w, cos, sin, perms, *, tm, out_dtype):
    n, d = x.shape
    period = cos.shape[1]
    nper = period // tm
    permute = perms is not None
    tab = (lambda i, j: (j, i % nper, 0)) if permute else (lambda i, j: (0, i % nper, 0))
    in_specs = [
        pl.BlockSpec((tm, d), lambda i, j: (i, 0)),
        pl.BlockSpec((1, d), lambda i, j: (0, 0)),
        pl.BlockSpec((d, GROUP_COLS), lambda i, j: (0, j)),
        pl.BlockSpec((1, tm, HEAD_DIM), tab),
        pl.BlockSpec((1, tm, HEAD_DIM), tab),
    ]
    args = [x, g.reshape(1, d), w, cos, sin]
    if permute:
        in_specs.append(pl.BlockSpec((N_GROUPS - 1, tm, tm), lambda i, j: (0, 0, 0)))
        args.append(perms)
    return pl.pallas_call(
        functools.partial(_norm_qkv_kernel, permute=permute),
        grid=(n // tm, N_GROUPS),
        in_specs=in_specs,
        out_specs=pl.BlockSpec((tm, GROUP_COLS), lambda i, j: (i, j)),
        out_shape=jax.ShapeDtypeStruct((n, QKV_COLS), out_dtype),
        scratch_shapes=[pltpu.VMEM((N_GROUPS if permute else 1, tm, d), BF16)],
        compiler_params=_cparams(("parallel", "arbitrary"), 48),
        name="norm_qkv",
    )(*args)


def _ffn_kernel(x_ref, g_ref, w1_ref, w2_ref, gf_ref, o_ref, xn_ref, acc_ref, *, final):
    f = pl.program_id(1)

    @pl.when(f == 0)
    def _():
        xn_ref[...] = _rms(x_ref[...], g_ref[...]).astype(BF16)
        acc_ref[...] = jnp.zeros_like(acc_ref)

    h = jnp.dot(xn_ref[...], w1_ref[...], preferred_element_type=F32)
    h = jnp.maximum(h, 0.0)
    h = (h * h).astype(BF16)
    acc_ref[...] += jnp.dot(h, w2_ref[...], preferred_element_type=F32)

    @pl.when(f == pl.num_programs(1) - 1)
    def _():
        y = x_ref[...] + acc_ref[...]
        if final:
            y = _rms(y, gf_ref[...])
        o_ref[...] = y


def ffn(x, g, w1, w2, gf, layer, *, tm, tf, final):
    n, d = x.shape
    dff = w1.shape[2]
    return pl.pallas_call(
        functools.partial(_ffn_kernel, final=final),
        grid=(n // tm, dff // tf),
        in_specs=[
            pl.BlockSpec((tm, d), lambda i, f: (i, 0)),
            pl.BlockSpec((1, d), lambda i, f: (0, 0)),
            pl.BlockSpec((None, d, tf), lambda i, f: (layer, 0, f)),
            pl.BlockSpec((None, tf, d), lambda i, f: (layer, f, 0)),
            pl.BlockSpec((1, d), lambda i, f: (0, 0)),
        ],
        out_specs=pl.BlockSpec((tm, d), lambda i, f: (i, 0)),
        out_shape=jax.ShapeDtypeStruct((n, d), F32),
        scratch_shapes=[pltpu.VMEM((tm, d), BF16), pltpu.VMEM((tm, d), F32)],
        compiler_params=_cparams(("parallel", "arbitrary"), 56),
        name="ffn",
    )(x, g.reshape(1, d), w1, w2, gf.reshape(1, d))


def _out_proj_kernel(a_ref, w_ref, b_ref, r_ref, o_ref):
    y = jnp.dot(a_ref[...].astype(BF16), w_ref[...], preferred_element_type=F32)
    o_ref[...] = r_ref[...] + (y + b_ref[...])


def out_proj(a, w, b, resid, *, tm):
    n, k = a.shape
    d = w.shape[1]
    return pl.pallas_call(
        _out_proj_kernel,
        grid=(n // tm,),
        in_specs=[
            pl.BlockSpec((tm, k), lambda i: (i, 0)),
            pl.BlockSpec((k, d), lambda i: (0, 0)),
            pl.BlockSpec((1, d), lambda i: (0, 0)),
            pl.BlockSpec((tm, d), lambda i: (i, 0)),
        ],
        out_specs=pl.BlockSpec((tm, d), lambda i: (i, 0)),
        out_shape=jax.ShapeDtypeStruct((n, d), F32),
        compiler_params=_cparams(("parallel",), 48),
        name="out_proj",
    )(a, w, b.reshape(1, d), resid)


def _unpermute_f32(pt, x):
    hi = x.astype(BF16)
    r1 = x - hi.astype(F32)
    mid = r1.astype(BF16)
    lo = (r1 - mid.astype(F32)).astype(BF16)
    out = jnp.dot(pt, hi, preferred_element_type=F32)
    out = out + jnp.dot(pt, mid, preferred_element_type=F32)
    return out + jnp.dot(pt, lo, preferred_element_type=F32)


def _merge_out_proj_kernel(o0_ref, o1_ref, o2_ref, l0_ref, l1_ref, l2_ref, pt_ref, w_ref, r_ref, o_ref, m_scr):
    outs = [o0_ref[...].astype(F32),
            jnp.dot(pt_ref[0], o1_ref[...], preferred_element_type=F32),
            jnp.dot(pt_ref[1], o2_ref[...], preferred_element_type=F32)]
    lses = [l0_ref[:, 0:ATT_HEADS],
            _unpermute_f32(pt_ref[0], l1_ref[...])[:, 0:ATT_HEADS],
            _unpermute_f32(pt_ref[1], l2_ref[...])[:, 0:ATT_HEADS]]
    mx = jnp.maximum(jnp.maximum(lses[0], lses[1]), lses[2])
    es = [jnp.exp(l - mx) for l in lses]
    inv = 1.0 / (es[0] + es[1] + es[2])
    ws = [e * inv for e in es]
    for h in range(ATT_HEADS):
        sl = slice(h * HEAD_DIM, (h + 1) * HEAD_DIM)
        acc = ws[0][:, h:h + 1] * outs[0][:, sl]
        acc = acc + ws[1][:, h:h + 1] * outs[1][:, sl]
        acc = acc + ws[2][:, h:h + 1] * outs[2][:, sl]
        m_scr[:, sl] = acc.astype(BF16)
    o_ref[...] = r_ref[...] + jnp.dot(m_scr[...], w_ref[...], preferred_element_type=F32)


def merge_out_proj(os_, ls_, perms_t, w, resid, *, tm):
    n = resid.shape[0]
    d = w.shape[1]
    ospec = pl.BlockSpec((tm, HD_ALL), lambda i: (i, 0))
    lspec = pl.BlockSpec((tm, LANES), lambda i: (i, 0))
    return pl.pallas_call(
        _merge_out_proj_kernel,
        grid=(n // tm,),
        in_specs=[ospec, ospec, ospec, lspec, lspec, lspec,
                  pl.BlockSpec((N_GROUPS - 1, tm, tm), lambda i: (0, 0, 0)),
                  pl.BlockSpec((HD_ALL, d), lambda i: (0, 0)),
                  pl.BlockSpec((tm, d), lambda i: (i, 0))],
        out_specs=pl.BlockSpec((tm, d), lambda i: (i, 0)),
        out_shape=jax.ShapeDtypeStruct((n, d), F32),
        scratch_shapes=[pltpu.VMEM((tm, HD_ALL), BF16)],
        compiler_params=_cparams(("parallel",), 48),
        name="merge_out_proj",
    )(*os_, *ls_, perms_t, w, resid)


def _band_slices(c):
    lo = max(c - 1, 0)
    hi = min(c + 2, N_CT)
    return (lo * LANES, hi * LANES), ((lo - (c - 1)) * LANES, (hi - (c - 1)) * LANES)


def _lru_ab(za, zx, u, sp):
    r = _sigmoid(za)
    i = _sigmoid(zx)
    log_a = (-LRU_C * r) * sp
    a = jnp.exp(log_a)
    b = jnp.sqrt(-jnp.tanh(log_a) * (a * a + 1.0)) * (i * u)
    return a, b


def _lru_gates(xg_tile_fn, wb_ref, c, u, gab, gxb, sp):
    (xlo, xhi), (rlo, rhi) = _band_slices(c)
    z = jnp.dot(xg_tile_fn(xlo, xhi), wb_ref[c, rlo:rhi, :], preferred_element_type=F32)
    return _lru_ab(z[:, :LANES] + gab, z[:, LANES:] + gxb, u, sp)


def _lru_mix_prompt_kernel(gu_ref, wb_ref, gab_ref, gxb_ref, lam_ref, hg_ref, hl_ref, hc_scr, *, tt):
    t = pl.program_id(1)
    nt = pl.num_programs(1)
    C = D_RNN
    SUB = 8

    @pl.when(t == 0)
    def _():
        hc_scr[...] = jnp.zeros((SUB, C), F32)

    sub = lax.broadcasted_iota(jnp.int32, (SUB, LANES), 0)
    keep = [sub >= d for d in (1, 2, 4)]
    sp = _softplus(-lam_ref[...])
    for c in range(N_CT):
        cs = slice(c * LANES, (c + 1) * LANES)
        ucs = slice(C + c * LANES, C + (c + 1) * LANES)
        (xlo, xhi), (rlo, rhi) = _band_slices(c)
        z = jnp.dot(gu_ref[:, C + xlo:C + xhi].astype(BF16), wb_ref[c, rlo:rhi, :], preferred_element_type=F32)
        hin = hc_scr[:, cs]
        for r0 in range(0, tt, 2 * SUB):
            hs = []
            for r1 in (r0, r0 + SUB):
                a, b = _lru_ab(z[r1:r1 + SUB, :LANES] + gab_ref[:, cs], z[r1:r1 + SUB, LANES:] + gxb_ref[:, cs],
                               gu_ref[r1:r1 + SUB, ucs], sp[:, cs])
                for d, kp in zip((1, 2, 4), keep):
                    a_sh = jnp.where(kp, pltpu.roll(a, d, 0), 1.0)
                    b_sh = jnp.where(kp, pltpu.roll(b, d, 0), 0.0)
                    b = a * b_sh + b
                    a = a * a_sh
                h = b + a * hin
                hin = jnp.broadcast_to(h[SUB - 1:SUB, :], (SUB, LANES))
                hs.append(h)
            h2 = jnp.concatenate(hs, axis=0)
            hg_ref[r0:r0 + 2 * SUB, cs] = (h2 * gu_ref[r0:r0 + 2 * SUB, cs]).astype(hg_ref.dtype)
        hc_scr[:, cs] = hin

    @pl.when(t == nt - 1)
    def _():
        hl_ref[0] = hc_scr[0:1, :]


def lru_mix_prompt(gu, wb, gab, gxb, lam, *, batch, seq, tt):
    C = D_RNN
    nt = seq // tt
    row = lambda v: v.reshape(1, C)
    full2 = lambda shape: pl.BlockSpec(shape, lambda b, t: (0,) * len(shape))
    return pl.pallas_call(
        functools.partial(_lru_mix_prompt_kernel, tt=tt),
        grid=(batch, nt),
        in_specs=[
            pl.BlockSpec((tt, 2 * C), lambda b, t: (b * nt + t, 0)),
            full2((N_CT, 3 * LANES, 2 * LANES)), full2((1, C)), full2((1, C)), full2((1, C)),
        ],
        out_specs=[
            pl.BlockSpec((tt, C), lambda b, t: (b * nt + t, 0)),
            pl.BlockSpec((1, 1, C), lambda b, t: (b, 0, 0)),
        ],
        out_shape=[
            jax.ShapeDtypeStruct((batch * seq, C), BF16),
            jax.ShapeDtypeStruct((batch, 1, C), F32),
        ],
        scratch_shapes=[pltpu.VMEM((8, C), F32)],
        compiler_params=_cparams(("arbitrary", "arbitrary"), 48),
        name="lru_mix_prompt",
    )(gu, wb, row(gab), row(gxb), row(lam))


def _lru_mix_sample_kernel(gu_ref, cst_ref, hp_ref, cw_ref, cb_ref, wb_ref, gab_ref, gxb_ref, lam_ref,
                           hg_ref, cso_ref, hl_ref, cv_scr, xg_scr, *, db, s_len):
    del db
    C = D_RNN
    W1 = LRU_CONV_W - 1

    def ext(j, cs):
        if j < W1:
            return cst_ref[j, :, cs]
        return gu_ref[j - W1, :, C + cs.start:C + cs.stop]

    for c in range(N_CT):
        cs = slice(c * LANES, (c + 1) * LANES)
        for s in range(s_len):
            acc = cw_ref[0:1, cs] * ext(s, cs)
            for k in range(1, LRU_CONV_W):
                acc = acc + cw_ref[k:k + 1, cs] * ext(s + k, cs)
            acc = acc + cb_ref[:, cs]
            cv_scr[s, :, cs] = acc
            xg_scr[s, :, cs] = acc.astype(BF16)
        for k in range(W1):
            cso_ref[k, :, cs] = ext(s_len + k, cs)

    sp = _softplus(-lam_ref[...])
    for c in range(N_CT):
        cs = slice(c * LANES, (c + 1) * LANES)
        h = hp_ref[:, cs]
        for s in range(s_len):
            a, b = _lru_gates(lambda lo, hi: xg_scr[s, :, lo:hi], wb_ref, c, cv_scr[s, :, cs],
                              gab_ref[:, cs], gxb_ref[:, cs], sp[:, cs])
            h = a * h + b
            hg_ref[s, :, cs] = h * gu_ref[s, :, cs]
        hl_ref[:, cs] = h


def lru_mix_sample(gu, cst, hp, cw, cb, wb, gab, gxb, lam, *, db, s_len):
    C = D_RNN
    row = lambda v: v.reshape(1, C)
    return pl.pallas_call(
        functools.partial(_lru_mix_sample_kernel, db=db, s_len=s_len),
        out_shape=[
            jax.ShapeDtypeStruct((s_len, db, C), F32),
            jax.ShapeDtypeStruct((LRU_CONV_W - 1, db, C), F32),
            jax.ShapeDtypeStruct((db, C), F32),
        ],
        scratch_shapes=[pltpu.VMEM((s_len, db, C), F32), pltpu.VMEM((s_len, db, C), BF16)],
        compiler_params=pltpu.CompilerParams(vmem_limit_bytes=48 * MIB),
        name="lru_mix_sample",
    )(gu, cst, hp, cw, row(cb), wb, row(gab), row(gxb), row(lam))


def _ln_silu(y, g, b):
    mu = jnp.mean(y, axis=-1, keepdims=True)
    yc = y - mu
    var = jnp.mean(yc * yc, axis=-1, keepdims=True)
    z = yc * lax.rsqrt(var + EPS) * g + b
    return z * _sigmoid(z)


def _cm_mix_prompt_kernel(u_ref, dw_ref, db_ref, lg_ref, lb_ref, o_ref, cs_ref, ubuf, cv_scr, *, tt):
    t = pl.program_id(1)
    nt = pl.num_programs(1)
    C = D_MODEL
    RB = 64
    HALO = 32
    OFF = HALO - (CM_KERNEL - 1)

    @pl.when(t == 0)
    def _():
        ubuf[0:HALO, :] = jnp.zeros((HALO, C), F32)

    ubuf[HALO:HALO + tt, :] = u_ref[...]

    WIN = RB + HALO
    for c in range(C // LANES):
        cs = slice(c * LANES, (c + 1) * LANES)
        for r0 in range(0, tt, RB):
            x = ubuf[r0:r0 + WIN, cs]
            acc = None
            for r in range(8):
                xr = x if r == 0 else pltpu.roll(x, WIN - r, 0)
                for o in range(OFF, OFF + CM_KERNEL):
                    if o % 8 != r:
                        continue
                    term = dw_ref[o - OFF:o - OFF + 1, cs] * xr[o - r:o - r + RB, :]
                    acc = term if acc is None else acc + term
            cv_scr[r0:r0 + RB, cs] = acc + db_ref[:, cs]

    @pl.when(t == nt - 1)
    def _():
        cs_ref[0] = ubuf[tt + OFF:tt + HALO, :]

    ubuf[0:HALO, :] = ubuf[tt:tt + HALO, :]

    for r0 in range(0, tt, RB):
        o_ref[r0:r0 + RB, :] = _ln_silu(cv_scr[r0:r0 + RB, :], lg_ref[...], lb_ref[...]).astype(o_ref.dtype)


def cm_mix_prompt(u, dw, db_, lg, lb, *, batch, seq, tt):
    C = D_MODEL
    nt = seq // tt
    row = lambda v: v.reshape(1, C)
    full2 = lambda shape: pl.BlockSpec(shape, lambda b, t: (0,) * len(shape))
    return pl.pallas_call(
        functools.partial(_cm_mix_prompt_kernel, tt=tt),
        grid=(batch, nt),
        in_specs=[
            pl.BlockSpec((tt, C), lambda b, t: (b * nt + t, 0)),
            full2((CM_KERNEL, C)), full2((1, C)), full2((1, C)), full2((1, C)),
        ],
        out_specs=[
            pl.BlockSpec((tt, C), lambda b, t: (b * nt + t, 0)),
            pl.BlockSpec((1, CM_KERNEL - 1, C), lambda b, t: (b, 0, 0)),
        ],
        out_shape=[
            jax.ShapeDtypeStruct((batch * seq, C), BF16),
            jax.ShapeDtypeStruct((batch, CM_KERNEL - 1, C), F32),
        ],
        scratch_shapes=[pltpu.VMEM((tt + 32, C), F32), pltpu.VMEM((tt, C), F32)],
        compiler_params=_cparams(("arbitrary", "arbitrary"), 48),
        name="cm_mix_prompt",
    )(u, dw, row(db_), row(lg), row(lb))


def _cm_mix_sample_kernel(u_ref, st_ref, dw_ref, db_ref, lg_ref, lb_ref, o_ref, so_ref, cv_scr, *, nb, s_len):
    del nb
    C = D_MODEL
    W1 = CM_KERNEL - 1

    def ext(j, cs):
        if j < W1:
            return st_ref[:, j, cs]
        return u_ref[j - W1, :, cs]

    for c in range(C // LANES):
        cs = slice(c * LANES, (c + 1) * LANES)
        for s in range(s_len):
            acc = dw_ref[0:1, cs] * ext(s, cs)
            for k in range(1, CM_KERNEL):
                acc = acc + dw_ref[k:k + 1, cs] * ext(s + k, cs)
            cv_scr[s, :, cs] = acc + db_ref[:, cs]
        for j in range(W1):
            so_ref[:, j, cs] = ext(j + s_len, cs)

    for s in range(s_len):
        o_ref[s] = _ln_silu(cv_scr[s], lg_ref[...], lb_ref[...])


def cm_mix_sample(u, st, dw, db_, lg, lb, *, db, s_len, nb):
    C = D_MODEL
    W1 = CM_KERNEL - 1
    row = lambda v: v.reshape(1, C)
    full1 = lambda shape: pl.BlockSpec(shape, lambda i: (0,) * len(shape))
    return pl.pallas_call(
        functools.partial(_cm_mix_sample_kernel, nb=nb, s_len=s_len),
        grid=(db // nb,),
        in_specs=[
            pl.BlockSpec((s_len, nb, C), lambda i: (0, i, 0)),
            pl.BlockSpec((nb, W1, C), lambda i: (i, 0, 0)),
            full1((CM_KERNEL, C)), full1((1, C)), full1((1, C)), full1((1, C)),
        ],
        out_specs=[
            pl.BlockSpec((s_len, nb, C), lambda i: (0, i, 0)),
            pl.BlockSpec((nb, W1, C), lambda i: (i, 0, 0)),
        ],
        out_shape=[
            jax.ShapeDtypeStruct((s_len, db, C), F32),
            jax.ShapeDtypeStruct((db, W1, C), F32),
        ],
        scratch_shapes=[pltpu.VMEM((s_len, nb, C), F32)],
        compiler_params=_cparams(("parallel",), 48),
        name="cm_mix_sample",
    )(u, st, dw, row(db_), row(lg), row(lb))


def _attn_prompt_kernel(q_ref, k_ref, v_ref, kp_ref, vp_ref, o_ref, l_ref, qs, kext, vext, os, ls, s_scr, *, ch):
    n = pl.program_id(2)
    qs[...] = q_ref[:, 0].reshape(ch, HD_ALL)
    kext[0:ATT_BLOCK, :] = kp_ref[:, 0].reshape(ATT_BLOCK, HD_ALL)
    kext[ATT_BLOCK:ATT_BLOCK + ch, :] = k_ref[:, 0].reshape(ch, HD_ALL)
    vext[0:ATT_BLOCK, :] = vp_ref[:, 0].reshape(ATT_BLOCK, HD_ALL)
    vext[ATT_BLOCK:ATT_BLOCK + ch, :] = v_ref[:, 0].reshape(ch, HD_ALL)
    qi = lax.broadcasted_iota(jnp.int32, (ATT_BLOCK, 2 * ATT_BLOCK), 0)
    kj = lax.broadcasted_iota(jnp.int32, (ATT_BLOCK, 2 * ATT_BLOCK), 1)
    band = jnp.logical_and(kj >= qi, kj <= qi + ATT_BLOCK)
    band_first = jnp.logical_and(band, jnp.logical_or(kj >= ATT_BLOCK, n > 0))
    lane = lax.broadcasted_iota(jnp.int32, (ATT_BLOCK, LANES), 1)
    dn = (((1,), (1,)), ((), ()))
    for i in range(ch // ATT_BLOCK):
        rows = slice(i * ATT_BLOCK, (i + 1) * ATT_BLOCK)
        krows = slice(i * ATT_BLOCK, (i + 2) * ATT_BLOCK)
        mask = band_first if i == 0 else band
        for h in range(ATT_HEADS):
            hs = slice(h * HEAD_DIM, (h + 1) * HEAD_DIM)
            s_scr[i % 2, h] = lax.dot_general(qs[rows, hs], kext[krows, hs], dn, preferred_element_type=F32)
        lse_tile = jnp.zeros((ATT_BLOCK, LANES), F32)
        for h in range(ATT_HEADS):
            hs = slice(h * HEAD_DIM, (h + 1) * HEAD_DIM)
            s = jnp.where(mask, s_scr[i % 2, h], -jnp.inf)
            m = jnp.max(jnp.maximum(s[:, :ATT_BLOCK], s[:, ATT_BLOCK:]), axis=-1, keepdims=True)
            p = jnp.exp2((s - m) * SCALE_LOG2E)
            l = jnp.sum(p[:, :ATT_BLOCK] + p[:, ATT_BLOCK:], axis=-1, keepdims=True)
            o = jnp.dot(p.astype(BF16), vext[krows, hs], preferred_element_type=F32)
            os[rows, hs] = (o * (1.0 / l)).astype(os.dtype)
            lse_tile = jnp.where(lane == h, m * SCALE + jnp.log(l), lse_tile)
        ls[rows, :] = lse_tile
    o_ref[:, 0] = os[...].reshape(o_ref.shape[0], o_ref.shape[2], HD_ALL)
    l_ref[:, 0] = ls[...].reshape(l_ref.shape[0], l_ref.shape[2], LANES)


def attn_prompt_group(qkv, g, *, batch, seq):
    win, dil = ATT_GROUPS[g]
    m = seq // dil
    ch = min(m, 512)
    nch = m // ch
    bpc = ch // ATT_BLOCK
    nblk = m // ATT_BLOCK
    rpt = QKV_TILE // dil
    ntc = ch // rpt
    prpt = min(rpt, ATT_BLOCK)
    ntiles = batch * seq // QKV_TILE
    cur_view = qkv.reshape(ntiles, dil, rpt, QKV_COLS)
    prev_view = qkv.reshape(ntiles * (rpt // prpt), dil, prpt, QKV_COLS)
    ncb = QKV_COLS // HD_ALL
    cq, ck, cv = 3 * g, 3 * g + 1, 3 * g + 2

    def cur(cb):
        return pl.BlockSpec((ntc, 1, rpt, HD_ALL), lambda b, r, n: (b * nch + n, r, 0, cb))

    def prev(cb):
        return pl.BlockSpec((ATT_BLOCK // prpt, 1, prpt, HD_ALL),
                            lambda b, r, n: (b * nblk + jnp.maximum(n * bpc - 1, 0), r, 0, cb))

    o, l = pl.pallas_call(
        functools.partial(_attn_prompt_kernel, ch=ch),
        grid=(batch, dil, nch),
        in_specs=[cur(cq), cur(ck), cur(cv), prev(ck), prev(cv)],
        out_specs=[pl.BlockSpec((ntc, 1, rpt, HD_ALL), lambda b, r, n: (b * nch + n, r, 0, 0)),
                   pl.BlockSpec((ntc, 1, rpt, LANES), lambda b, r, n: (b * nch + n, r, 0, 0))],
        out_shape=[jax.ShapeDtypeStruct((ntiles, dil, rpt, HD_ALL), BF16),
                   jax.ShapeDtypeStruct((ntiles, dil, rpt, LANES), F32)],
        scratch_shapes=[pltpu.VMEM((ch, HD_ALL), BF16),
                        pltpu.VMEM((ATT_BLOCK + ch, HD_ALL), BF16), pltpu.VMEM((ATT_BLOCK + ch, HD_ALL), BF16),
                        pltpu.VMEM((ch, HD_ALL), BF16), pltpu.VMEM((ch, LANES), F32),
                        pltpu.VMEM((2, ATT_HEADS, ATT_BLOCK, 2 * ATT_BLOCK), F32)],
        compiler_params=_cparams(("parallel", "parallel", "arbitrary"), 48),
        name=f"attn_prompt_g{g}",
    )(cur_view, cur_view, cur_view, prev_view, prev_view)
    return o.reshape(batch * seq, HD_ALL), l.reshape(batch * seq, LANES)


QKV_TILES = QKV_COLS // LANES
GROUP_TILES = GROUP_COLS // LANES


def _attn_sample_kernel(qkv_ref, c0_ref, c1_ref, c2_ref, o_ref, s_scr, p_scr, *, s_len):
    nk = ATT_BLOCK * ATT_HEADS
    caches = (c0_ref, c1_ref, c2_ref)
    col = lax.broadcasted_iota(jnp.int32, (ATT_HEADS, nk), 1)
    rowh = lax.broadcasted_iota(jnp.int32, (ATT_HEADS, nk), 0)
    own = (col & (ATT_HEADS - 1)) == rowh
    key = col >> 3
    dn = (((1,), (1,)), ((), ()))

    def kvflat(g, s, which):
        r = 0 if g == 0 else s
        return caches[g][0, :, r, which].reshape(nk, HEAD_DIM).astype(BF16)

    def tile(s, g, which):
        lo = g * GROUP_TILES + which * ATT_HEADS
        return qkv_ref[0, s, lo:lo + ATT_HEADS, :]

    def rnd(x):
        return x.astype(BF16).astype(F32)

    q0 = qkv_ref[0, :, 0:ATT_HEADS, :].reshape(s_len * ATT_HEADS, HEAD_DIM).astype(BF16)
    s0 = lax.dot_general(q0, kvflat(0, 0, 0), dn, preferred_element_type=F32)
    s_scr[0:s_len] = s0.reshape(s_len, ATT_HEADS, nk)
    for g in range(1, N_GROUPS):
        for s in range(s_len):
            s_scr[g * s_len + s] = lax.dot_general(tile(s, g, 0).astype(BF16), kvflat(g, s, 0), dn,
                                                   preferred_element_type=F32)

    stats = {}
    for g in range(N_GROUPS):
        for s in range(s_len):
            idx = g * s_len + s
            valid = jnp.logical_and(own, key >= s) if g == 0 else own
            sc = jnp.where(valid, s_scr[idx], -jnp.inf)
            news = tuple(range(s + 1)) if g == 0 else (s,)
            qf = rnd(tile(s, g, 0))
            sn = [jnp.sum(qf * rnd(tile(j, g, 1)), axis=-1, keepdims=True) for j in news]
            m = jnp.max(sc, axis=-1, keepdims=True)
            for x in sn:
                m = jnp.maximum(m, x)
            p = jnp.exp2((sc - m) * SCALE_LOG2E)
            pn = [jnp.exp2((x - m) * SCALE_LOG2E) for x in sn]
            l = jnp.sum(p, axis=-1, keepdims=True)
            for x in pn:
                l = l + x
            p_scr[idx] = p
            stats[(g, s)] = (m, l, pn, news)

    outs = {}
    p0 = p_scr[0:s_len].reshape(s_len * ATT_HEADS, nk).astype(BF16)
    o0 = jnp.dot(p0, kvflat(0, 0, 1), preferred_element_type=F32).reshape(s_len, ATT_HEADS, HEAD_DIM)
    for g in range(N_GROUPS):
        for s in range(s_len):
            idx = g * s_len + s
            m, l, pn, news = stats[(g, s)]
            if g == 0:
                o = o0[s]
            else:
                o = jnp.dot(p_scr[idx].astype(BF16), kvflat(g, s, 1), preferred_element_type=F32)
            for x, j in zip(pn, news):
                o = o + rnd(x) * rnd(tile(j, g, 2))
            outs[(g, s)] = (o * (1.0 / l), m * SCALE + jnp.log(l))

    for s in range(s_len):
        lses = [outs[(g, s)][1] for g in range(N_GROUPS)]
        mx = jnp.maximum(jnp.maximum(lses[0], lses[1]), lses[2])
        es = [jnp.exp(x - mx) for x in lses]
        inv = 1.0 / (es[0] + es[1] + es[2])
        o_ref[0, s] = ((es[0] * inv) * outs[(0, s)][0] + (es[1] * inv) * outs[(1, s)][0]
                       + (es[2] * inv) * outs[(2, s)][0])


def attn_sample(qkv, c0, c1, c2, layer, *, db, s_len):
    nk = ATT_BLOCK * ATT_HEADS

    def cspec(c):
        r = min(c.shape[2], s_len)
        return pl.BlockSpec((1, ATT_BLOCK, r, 2, ATT_HEADS, HEAD_DIM), lambda b: (layer * db + b, 0, 0, 0, 0, 0))

    return pl.pallas_call(
        functools.partial(_attn_sample_kernel, s_len=s_len),
        grid=(db,),
        in_specs=[pl.BlockSpec((1, s_len, QKV_TILES, LANES), lambda b: (b, 0, 0, 0)),
                  cspec(c0), cspec(c1), cspec(c2)],
        out_specs=pl.BlockSpec((1, s_len, ATT_HEADS, HEAD_DIM), lambda b: (b, 0, 0, 0)),
        out_shape=jax.ShapeDtypeStruct((db, s_len, ATT_HEADS, HEAD_DIM), F32),
        scratch_shapes=[pltpu.VMEM((N_GROUPS * s_len, ATT_HEADS, nk), F32),
                        pltpu.VMEM((N_GROUPS * s_len, ATT_HEADS, nk), F32)],
        compiler_params=_cparams(("arbitrary",), 56),
        name="attn_sample",
    )(qkv, c0, c1, c2)


def _gate_band(ga_w, gx_w):
    eye = jnp.eye(LRU_BLOCKS, dtype=F32)

    def dense(w):
        d = jnp.einsum('ncd,nm->ncmd', w, eye).reshape(D_RNN, D_RNN)
        return jnp.pad(d, ((LANES, LANES), (0, 0)))

    da, dx = dense(ga_w), dense(gx_w)
    bands = [jnp.concatenate([da[c * LANES:(c + 3) * LANES, c * LANES:(c + 1) * LANES],
                              dx[c * LANES:(c + 3) * LANES, c * LANES:(c + 1) * LANES]], axis=1)
             for c in range(N_CT)]
    return jnp.stack(bands).astype(BF16)


def _tile_order(dil):
    rows = np.arange(QKV_TILE)
    rpt = QKV_TILE // dil
    return (rows % rpt) * dil + rows // rpt


def _perm_matrices():
    mats = []
    for _, dil in ATT_GROUPS[1:]:
        p = np.zeros((QKV_TILE, QKV_TILE), np.float32)
        p[np.arange(QKV_TILE), _tile_order(dil)] = 1.0
        mats.append(p)
    return np.stack(mats)


def _rope_tables(pos):
    half = HEAD_DIM // 2
    inv_freq = ROPE_THETA ** (-jnp.arange(half, dtype=F32) / half)
    ang = pos.astype(F32)[:, None] * inv_freq
    cos, sin = jnp.cos(ang), jnp.sin(ang)
    return jnp.concatenate([cos, cos], axis=-1), jnp.concatenate([-sin, sin], axis=-1)


def _row_tile(n, cap):
    t = cap
    while n % t:
        t //= 2
    return t


def kernel(x_prompt, x_sample, state_lru_conv, state_lru_h, state_cm_conv, cache_kv_w128, cache_kv_w512, cache_kv_w2048, norm_mix, norm_ffn, norm_final, lru_w_in, lru_conv_w, lru_conv_b, lru_gate_a_w, lru_gate_a_b, lru_gate_x_w, lru_gate_x_b, lru_lambda, lru_w_out, cm_w_pw1, cm_b_pw1, cm_dw_w, cm_dw_b, cm_ln_g, cm_ln_b, cm_w_pw2, cm_b_pw2, att_w_qkv, att_w_o, ffn_w1, ffn_w2):
    B, T, D = x_prompt.shape
    DB, S, _ = x_sample.shape
    depth = norm_mix.shape[0]
    NP, NS = B * T, DB * S
    assert D == D_MODEL and T % (ATT_GROUPS[-1][1] * ATT_BLOCK) == 0 and T % QKV_TILE == 0 and S == 4
    for c, (win, _) in zip((cache_kv_w128, cache_kv_w512, cache_kv_w2048), ATT_GROUPS):
        assert c.shape[2] == win
    tmp = _row_tile(NP, 1024)
    tms = _row_tile(NS, 512)
    zeros_d = jnp.zeros((D,), F32)

    xp = x_prompt.reshape(NP, D)
    xs = jnp.swapaxes(x_sample, 0, 1).reshape(NS, D)

    pos_p = np.arange(T).reshape(T // QKV_TILE, QKV_TILE)
    pos_g = np.stack([pos_p[:, _tile_order(dil)].reshape(T) for _, dil in ATT_GROUPS])
    cos_p, sin_p = _rope_tables(jnp.asarray(pos_g.reshape(-1)))
    cos_p, sin_p = cos_p.reshape(N_GROUPS, T, HEAD_DIM), sin_p.reshape(N_GROUPS, T, HEAD_DIM)
    pm = _perm_matrices()
    perms = jnp.asarray(pm, BF16)
    perms_t = jnp.asarray(np.swapaxes(pm, 1, 2), BF16)
    cos_s, sin_s = _rope_tables(PAST_LEN + jnp.arange(NS) // DB)

    ffn_w1b, ffn_w2b = ffn_w1.astype(BF16), ffn_w2.astype(BF16)

    p_lru_conv, p_lru_h, s_lru_conv, s_lru_h = [], [], [], []
    p_cm_conv, s_cm_conv = [], []
    p_kv = [[] for _ in ATT_GROUPS]
    s_kv = [[] for _ in ATT_GROUPS]

    for i in range(depth):
        m, j = i % 3, i // 3
        if m == 0:
            w_in = lru_w_in[j].astype(BF16)
            w_out = lru_w_out[j].astype(BF16)
            wb = _gate_band(lru_gate_a_w[j], lru_gate_x_w[j])
            lp = (lru_conv_w[j], lru_conv_b[j], wb, lru_gate_a_b[j], lru_gate_x_b[j], lru_lambda[j])
            gu, cp = norm_proj_gelu(xp, norm_mix[i], w_in, (lru_conv_w[j], lru_conv_b[j], T), tm=_row_tile(T, 512))
            hg, hp = lru_mix_prompt(gu, wb, lru_gate_a_b[j], lru_gate_x_b[j], lru_lambda[j], batch=B, seq=T, tt=512)
            xp = out_proj(hg, w_out, zeros_d, xp, tm=tmp)
            p_lru_conv.append(cp)
            p_lru_h.append(hp.reshape(B, D_RNN))
            gus = norm_proj_gelu(xs, norm_mix[i], w_in, tm=tms)
            hgs, cs, hs = lru_mix_sample(gus.reshape(S, DB, 2 * D_RNN), jnp.swapaxes(state_lru_conv[j], 0, 1),
                                         state_lru_h[j], *lp, db=DB, s_len=S)
            xs = out_proj(hgs.reshape(NS, D_RNN), w_out, zeros_d, xs, tm=tms)
            s_lru_conv.append(jnp.swapaxes(cs, 0, 1))
            s_lru_h.append(hs)
        elif m == 1:
            w1 = cm_w_pw1[j].astype(BF16)
            w2 = cm_w_pw2[j].astype(BF16)
            cp_ = (cm_dw_w[j], cm_dw_b[j], cm_ln_g[j], cm_ln_b[j])
            u = norm_glu(xp, norm_mix[i], w1, cm_b_pw1[j], tm=tmp, tn=512)
            y, cp = cm_mix_prompt(u, *cp_, batch=B, seq=T, tt=512)
            xp = out_proj(y, w2, cm_b_pw2[j], xp, tm=tmp)
            p_cm_conv.append(cp)
            us = norm_glu(xs, norm_mix[i], w1, cm_b_pw1[j], tm=tms, tn=512)
            nb = _row_tile(DB, 32)
            ys, cs = cm_mix_sample(us.reshape(S, DB, D), state_cm_conv[j], *cp_,
                                   db=DB, s_len=S, nb=nb)
            xs = out_proj(ys.reshape(NS, D), w2, cm_b_pw2[j], xs, tm=tms)
            s_cm_conv.append(cs)
        else:
            wqkv = att_w_qkv[j].astype(BF16)
            wo = att_w_o[j].astype(BF16)
            qkv = norm_qkv(xp, norm_mix[i], wqkv, cos_p, sin_p, perms, tm=QKV_TILE, out_dtype=BF16)
            og = [attn_prompt_group(qkv, g, batch=B, seq=T) for g in range(N_GROUPS)]
            xp = merge_out_proj([o for o, _ in og], [l for _, l in og], perms_t, wo, xp, tm=QKV_TILE)
            for g, (win, dil) in enumerate(ATT_GROUPS):
                keep = min(win, T)
                kvg = qkv.reshape(B, T, QKV_COLS)[:, T - keep:, g * GROUP_COLS + HD_ALL:(g + 1) * GROUP_COLS]
                if dil > 1:
                    kvg = kvg.reshape(B, keep // QKV_TILE, dil, QKV_TILE // dil, 2 * HD_ALL)
                    kvg = jnp.swapaxes(kvg, 2, 3)
                p_kv[g].append(kvg.astype(F32).reshape(B, keep, 2, ATT_HEADS, HEAD_DIM))
            qkvs = norm_qkv(xs, norm_mix[i], wqkv, cos_s[None], sin_s[None], None, tm=tms, out_dtype=F32)
            qkvs3 = jnp.swapaxes(qkvs.reshape(S, DB, QKV_COLS), 0, 1)
            nl = cache_kv_w128.shape[0]
            caches = [c.reshape(nl * DB, ATT_BLOCK, c.shape[2] // ATT_BLOCK, 2, ATT_HEADS, HEAD_DIM)
                      for c in (cache_kv_w128, cache_kv_w512, cache_kv_w2048)]
            os_ = attn_sample(qkvs3.reshape(DB, S, QKV_TILES, LANES), *caches, j, db=DB, s_len=S)
            xs = out_proj(jnp.swapaxes(os_.reshape(DB, S, HD_ALL), 0, 1).reshape(NS, HD_ALL), wo, zeros_d, xs, tm=tms)
            for g in range(N_GROUPS):
                kvg = qkvs3[:, :, g * GROUP_COLS + HD_ALL:(g + 1) * GROUP_COLS]
                s_kv[g].append(kvg.reshape(DB, S, 2, ATT_HEADS, HEAD_DIM))
        final = i == depth - 1
        xp = ffn(xp, norm_ffn[i], ffn_w1b, ffn_w2b, norm_final, i, tm=tmp, tf=1024, final=final)
        xs = ffn(xs, norm_ffn[i], ffn_w1b, ffn_w2b, norm_final, i, tm=tms, tf=1024, final=final)

    return (xp.reshape(B, T, D), jnp.swapaxes(xs.reshape(S, DB, D), 0, 1),
            jnp.stack(p_lru_conv), jnp.stack(p_lru_h), jnp.stack(p_cm_conv),
            jnp.stack(p_kv[0]), jnp.stack(p_kv[1]), jnp.stack(p_kv[2]),
            jnp.stack(s_lru_conv), jnp.stack(s_lru_h), jnp.stack(s_cm_conv),
            jnp.stack(s_kv[0]), jnp.stack(s_kv[1]), jnp.stack(s_kv[2]))
```

```python
import functools

import numpy as np
import jax
import jax.numpy as jnp
from jax import lax
from jax.experimental import pallas as pl
from jax.experimental.pallas import tpu as pltpu

F32 = jnp.float32
BF16 = jnp.bfloat16

LANES = 128
D_MODEL = 1024
D_RNN = 1408
LRU_BLOCKS = 16
LRU_BLOCK_DIM = D_RNN // LRU_BLOCKS
LRU_CONV_W = 4
LRU_C = 8.0
CM_KERNEL = 31
ATT_GROUPS = ((128, 1), (512, 4), (2048, 16))
N_GROUPS = 3
ATT_HEADS = 8
HEAD_DIM = 128
ATT_BLOCK = 128
ROPE_THETA = 10000.0
PAST_LEN = 2048
EPS = 1e-6
QKV_COLS = N_GROUPS * 3 * ATT_HEADS * HEAD_DIM
GROUP_COLS = 3 * ATT_HEADS * HEAD_DIM
HD_ALL = ATT_HEADS * HEAD_DIM
QKV_TILE = 512
N_CT = D_RNN // LANES
SCALE = HEAD_DIM ** -0.5
SCALE_LOG2E = SCALE * 1.4426950408889634
MIB = 1024 * 1024


def _cparams(sem, vmem_mib):
    return pltpu.CompilerParams(dimension_semantics=sem, vmem_limit_bytes=vmem_mib * MIB)


def _rms(x, g):
    ms = jnp.mean(x * x, axis=-1, keepdims=True)
    return (x * lax.rsqrt(ms + EPS)) * g


def _sigmoid(x):
    return 1.0 / (1.0 + jnp.exp(-x))


def _softplus(x):
    return jnp.maximum(x, 0.0) + jnp.log1p(jnp.exp(-jnp.abs(x)))


def _gelu_tanh(x):
    return x * (0.5 * (1.0 + jnp.tanh(0.7978845608028654 * (x + 0.044715 * (x * x * x)))))


def _norm_proj_gelu_kernel(x_ref, g_ref, w_ref, *rest, half, rc, conv_tiles):
    tm = x_ref.shape[0]
    if conv_tiles:
        cw_ref, cb_ref, o_ref, cs_ref, ubuf = rest
        i = pl.program_id(0)

        @pl.when(i % conv_tiles == 0)
        def _():
            ubuf[0:8, :] = jnp.zeros((8, half), F32)
    else:
        (o_ref,) = rest
    for r0 in range(0, tm, rc):
        xn = _rms(x_ref[r0:r0 + rc, :], g_ref[...]).astype(BF16)
        o_ref[r0:r0 + rc, 0:half] = _gelu_tanh(jnp.dot(xn, w_ref[:, 0:half], preferred_element_type=F32))
        u = jnp.dot(xn, w_ref[:, half:2 * half], preferred_element_type=F32)
        if not conv_tiles:
            o_ref[r0:r0 + rc, half:2 * half] = u
            continue
        ubuf[8 + r0:8 + r0 + rc, :] = u
        for c in range(half // LANES):
            cs = slice(c * LANES, (c + 1) * LANES)
            for q0 in range(r0, r0 + rc, 64):
                acc = cw_ref[0:1, cs] * ubuf[q0 + 5:q0 + 69, cs]
                for k in range(1, LRU_CONV_W):
                    acc = acc + cw_ref[k:k + 1, cs] * ubuf[q0 + 5 + k:q0 + 69 + k, cs]
                o_ref[q0:q0 + 64, half + c * LANES:half + (c + 1) * LANES] = acc + cb_ref[:, cs]
    if conv_tiles:
        @pl.when(i % conv_tiles == conv_tiles - 1)
        def _():
            cs_ref[0] = ubuf[tm + 5:tm + 8, :]

        ubuf[0:8, :] = ubuf[tm:tm + 8, :]


def norm_proj_gelu(x, g, w, conv=None, *, tm):
    n, d = x.shape
    nout = w.shape[1]
    half = nout // 2
    in_specs = [
        pl.BlockSpec((tm, d), lambda i: (i, 0)),
        pl.BlockSpec((1, d), lambda i: (0, 0)),
        pl.BlockSpec((d, nout), lambda i: (0, 0)),
    ]
    args = [x, g.reshape(1, d), w]
    out_specs = pl.BlockSpec((tm, nout), lambda i: (i, 0))
    out_shape = jax.ShapeDtypeStruct((n, nout), F32)
    scratch = []
    conv_tiles = 0
    if conv is not None:
        cw, cb, seq = conv
        conv_tiles = seq // tm
        in_specs += [pl.BlockSpec((LRU_CONV_W, half), lambda i: (0, 0)), pl.BlockSpec((1, half), lambda i: (0, 0))]
        args += [cw, cb.reshape(1, half)]
        out_specs = [out_specs, pl.BlockSpec((1, LRU_CONV_W - 1, half), lambda i: (i // conv_tiles, 0, 0))]
        out_shape = [out_shape, jax.ShapeDtypeStruct((n // seq, LRU_CONV_W - 1, half), F32)]
        scratch = [pltpu.VMEM((tm + 8, half), F32)]
    return pl.pallas_call(
        functools.partial(_norm_proj_gelu_kernel, half=half, rc=min(tm, 128), conv_tiles=conv_tiles),
        grid=(n // tm,),
        in_specs=in_specs,
        out_specs=out_specs,
        out_shape=out_shape,
        scratch_shapes=scratch,
        compiler_params=_cparams(("arbitrary",), 48),
        name="norm_proj_gelu",
    )(*args)


def _norm_glu_kernel(x_ref, g_ref, wa_ref, wg_ref, ba_ref, bg_ref, o_ref, xn_ref):
    @pl.when(pl.program_id(1) == 0)
    def _():
        xn_ref[...] = _rms(x_ref[...], g_ref[...]).astype(BF16)

    xn = xn_ref[...]
    a = jnp.dot(xn, wa_ref[...], preferred_element_type=F32) + ba_ref[...]
    gate = jnp.dot(xn, wg_ref[...], preferred_element_type=F32) + bg_ref[...]
    o_ref[...] = a * _sigmoid(gate)


def norm_glu(x, g, w, b, *, tm, tn):
    n, d = x.shape
    half = w.shape[1] // 2
    nb = half // tn
    b2 = b.reshape(1, 2 * half)
    return pl.pallas_call(
        _norm_glu_kernel,
        grid=(n // tm, nb),
        in_specs=[
            pl.BlockSpec((tm, d), lambda i, j: (i, 0)),
            pl.BlockSpec((1, d), lambda i, j: (0, 0)),
            pl.BlockSpec((d, tn), lambda i, j: (0, j)),
            pl.BlockSpec((d, tn), lambda i, j: (0, j + nb)),
            pl.BlockSpec((1, tn), lambda i, j: (0, j)),
            pl.BlockSpec((1, tn), lambda i, j: (0, j + nb)),
        ],
        out_specs=pl.BlockSpec((tm, tn), lambda i, j: (i, j)),
        out_shape=jax.ShapeDtypeStruct((n, half), F32),
        scratch_shapes=[pltpu.VMEM((tm, d), BF16)],
        compiler_params=_cparams(("parallel", "arbitrary"), 48),
        name="norm_glu",
    )(x, g.reshape(1, d), w, w, b2, b2)


def _norm_qkv_kernel(x_ref, g_ref, w_ref, cos_ref, sin_ref, *rest, permute):
    if permute:
        p_ref, o_ref, xn_ref = rest
    else:
        o_ref, xn_ref = rest
    j = pl.program_id(1)

    @pl.when(j == 0)
    def _():
        xn = _rms(x_ref[...], g_ref[...]).astype(BF16)
        xn_ref[0] = xn
        if permute:
            for g in range(1, N_GROUPS):
                xn_ref[g] = jnp.dot(p_ref[g - 1], xn, preferred_element_type=F32).astype(BF16)

    xn = xn_ref[j] if permute else xn_ref[0]
    cos = cos_ref[0]
    sin = sin_ref[0]
    for part in range(3):
        acc = jnp.dot(xn, w_ref[:, part * HD_ALL:(part + 1) * HD_ALL], preferred_element_type=F32)
        for h in range(ATT_HEADS):
            hs = slice(h * HEAD_DIM, (h + 1) * HEAD_DIM)
            xh = acc[:, hs]
            if part < 2:
                xh = xh * cos + pltpu.roll(xh, HEAD_DIM // 2, 1) * sin
            o_ref[:, part * HD_ALL + h * HEAD_DIM:part * HD_ALL + (h + 1) * HEAD_DIM] = xh.astype(o_ref.dtype)


def norm_qkv(x, g, w, cos, sin, perms, *, tm, out_dtype):
    n, d = x.shape
    period = cos.shape[1]
    nper = period // tm
    permute = perms is not None
    tab = (lambda i, j: (j, i % nper, 0)) if permute else (lambda i, j: (0, i % nper, 0))
    in_specs = [
        pl.BlockSpec((tm, d), lambda i, j: (i, 0)),
        pl.BlockSpec((1, d), lambda i, j: (0, 0)),
        pl.BlockSpec((d, GROUP_COLS), lambda i, j: (0, j)),
        pl.BlockSpec((1, tm, HEAD_DIM), tab),
        pl.BlockSpec((1, tm, HEAD_DIM), tab),
    ]
    args = [x, g.reshape(1, d), w, cos, sin]
    if permute:
        in_specs.append(pl.BlockSpec((N_GROUPS - 1, tm, tm), lambda i, j: (0, 0, 0)))
        args.append(perms)
    return pl.pallas_call(
        functools.partial(_norm_qkv_kernel, permute=permute),
        grid=(n // tm, N_GROUPS),
        in_specs=in_specs,
        out_specs=pl.BlockSpec((tm, GROUP_COLS), lambda i, j: (i, j)),
        out_shape=jax.ShapeDtypeStruct((n, QKV_COLS), out_dtype),
        scratch_shapes=[pltpu.VMEM((N_GROUPS if permute else 1, tm, d), BF16)],
        compiler_params=_cparams(("parallel", "arbitrary"), 48),
        name="norm_qkv",
    )(*args)


def _ffn_kernel(x_ref, g_ref, w1_ref, w2_ref, gf_ref, o_ref, xn_ref, acc_ref, *, final):
    f = pl.program_id(1)

    @pl.when(f == 0)
    def _():
        xn_ref[...] = _rms(x_ref[...], g_ref[...]).astype(BF16)
        acc_ref[...] = jnp.zeros_like(acc_ref)

    h = jnp.dot(xn_ref[...], w1_ref[...], preferred_element_type=F32)
    h = jnp.maximum(h, 0.0)
    h = (h * h).astype(BF16)
    acc_ref[...] += jnp.dot(h, w2_ref[...], preferred_element_type=F32)

    @pl.when(f == pl.num_programs(1) - 1)
    def _():
        y = x_ref[...] + acc_ref[...]
        if final:
            y = _rms(y, gf_ref[...])
        o_ref[...] = y


def ffn(x, g, w1, w2, gf, layer, *, tm, tf, final):
    n, d = x.shape
    dff = w1.shape[2]
    return pl.pallas_call(
        functools.partial(_ffn_kernel, final=final),
        grid=(n // tm, dff // tf),
        in_specs=[
            pl.BlockSpec((tm, d), lambda i, f: (i, 0)),
            pl.BlockSpec((1, d), lambda i, f: (0, 0)),
            pl.BlockSpec((None, d, tf), lambda i, f: (layer, 0, f)),
            pl.BlockSpec((None, tf, d), lambda i, f: (layer, f, 0)),
            pl.BlockSpec((1, d), lambda i, f: (0, 0)),
        ],
        out_specs=pl.BlockSpec((tm, d), lambda i, f: (i, 0)),
        out_shape=jax.ShapeDtypeStruct((n, d), F32),
        scratch_shapes=[pltpu.VMEM((tm, d), BF16), pltpu.VMEM((tm, d), F32)],
        compiler_params=_cparams(("parallel", "arbitrary"), 56),
        name="ffn",
    )(x, g.reshape(1, d), w1, w2, gf.reshape(1, d))


def _out_proj_kernel(a_ref, w_ref, b_ref, r_ref, o_ref):
    y = jnp.dot(a_ref[...].astype(BF16), w_ref[...], preferred_element_type=F32)
    o_ref[...] = r_ref[...] + (y + b_ref[...])


def out_proj(a, w, b, resid, *, tm):
    n, k = a.shape
    d = w.shape[1]
    return pl.pallas_call(
        _out_proj_kernel,
        grid=(n // tm,),
        in_specs=[
            pl.BlockSpec((tm, k), lambda i: (i, 0)),
            pl.BlockSpec((k, d), lambda i: (0, 0)),
            pl.BlockSpec((1, d), lambda i: (0, 0)),
            pl.BlockSpec((tm, d), lambda i: (i, 0)),
        ],
        out_specs=pl.BlockSpec((tm, d), lambda i: (i, 0)),
        out_shape=jax.ShapeDtypeStruct((n, d), F32),
        compiler_params=_cparams(("parallel",), 48),
        name="out_proj",
    )(a, w, b.reshape(1, d), resid)


def _unpermute_f32(pt, x):
    hi = x.astype(BF16)
    r1 = x - hi.astype(F32)
    mid = r1.astype(BF16)
    lo = (r1 - mid.astype(F32)).astype(BF16)
    out = jnp.dot(pt, hi, preferred_element_type=F32)
    out = out + jnp.dot(pt, mid, preferred_element_type=F32)
    return out + jnp.dot(pt, lo, preferred_element_type=F32)


def _merge_out_proj_kernel(o0_ref, o1_ref, o2_ref, l0_ref, l1_ref, l2_ref, pt_ref, w_ref, r_ref, o_ref, m_scr):
    outs = [o0_ref[...].astype(F32),
            jnp.dot(pt_ref[0], o1_ref[...], preferred_element_type=F32),
            jnp.dot(pt_ref[1], o2_ref[...], preferred_element_type=F32)]
    lses = [l0_ref[:, 0:ATT_HEADS],
            _unpermute_f32(pt_ref[0], l1_ref[...])[:, 0:ATT_HEADS],
            _unpermute_f32(pt_ref[1], l2_ref[...])[:, 0:ATT_HEADS]]
    mx = jnp.maximum(jnp.maximum(lses[0], lses[1]), lses[2])
    es = [jnp.exp(l - mx) for l in lses]
    inv = 1.0 / (es[0] + es[1] + es[2])
    ws = [e * inv for e in es]
    for h in range(ATT_HEADS):
        sl = slice(h * HEAD_DIM, (h + 1) * HEAD_DIM)
        acc = ws[0][:, h:h + 1] * outs[0][:, sl]
        acc = acc + ws[1][:, h:h + 1] * outs[1][:, sl]
        acc = acc + ws[2][:, h:h + 1] * outs[2][:, sl]
        m_scr[:, sl] = acc.astype(BF16)
    o_ref[...] = r_ref[...] + jnp.dot(m_scr[...], w_ref[...], preferred_element_type=F32)


def merge_out_proj(os_, ls_, perms_t, w, resid, *, tm):
    n = resid.shape[0]
    d = w.shape[1]
    ospec = pl.BlockSpec((tm, HD_ALL), lambda i: (i, 0))
    lspec = pl.BlockSpec((tm, LANES), lambda i: (i, 0))
    return pl.pallas_call(
        _merge_out_proj_kernel,
        grid=(n // tm,),
        in_specs=[ospec, ospec, ospec, lspec, lspec, lspec,
                  pl.BlockSpec((N_GROUPS - 1, tm, tm), lambda i: (0, 0, 0)),
                  pl.BlockSpec((HD_ALL, d), lambda i: (0, 0)),
                  pl.BlockSpec((tm, d), lambda i: (i, 0))],
        out_specs=pl.BlockSpec((tm, d), lambda i: (i, 0)),
        out_shape=jax.ShapeDtypeStruct((n, d), F32),
        scratch_shapes=[pltpu.VMEM((tm, HD_ALL), BF16)],
        compiler_params=_cparams(("parallel",), 48),
        name="merge_out_proj",
    )(*os_, *ls_, perms_t, w, resid)


def _band_slices(c):
    lo = max(c - 1, 0)
    hi = min(c + 2, N_CT)
    return (lo * LANES, hi * LANES), ((lo - (c - 1)) * LANES, (hi - (c - 1)) * LANES)


def _lru_ab(za, zx, u, sp):
    r = _sigmoid(za)
    i = _sigmoid(zx)
    log_a = r * sp
    a = jnp.exp(log_a)
    b = jnp.sqrt(-jnp.tanh(log_a) * (a * a + 1.0)) * (i * u)
    return a, b


def _lru_gates(xg_tile_fn, wb_ref, c, u, gab, gxb, sp):
    (xlo, xhi), (rlo, rhi) = _band_slices(c)
    z = jnp.dot(xg_tile_fn(xlo, xhi), wb_ref[c, rlo:rhi, :], preferred_element_type=F32)
    return _lru_ab(z[:, :LANES] + gab, z[:, LANES:] + gxb, u, sp)


def _lru_mix_prompt_kernel(gu_ref, wb_ref, gab_ref, gxb_ref, lam_ref, hg_ref, hl_ref, hc_scr, *, tt):
    t = pl.program_id(1)
    nt = pl.num_programs(1)
    C = D_RNN
    SUB = 8

    @pl.when(t == 0)
    def _():
        hc_scr[...] = jnp.zeros((SUB, C), F32)

    sub = lax.broadcasted_iota(jnp.int32, (SUB, LANES), 0)
    keep = [sub >= d for d in (1, 2, 4)]
    sp = -LRU_C * _softplus(-lam_ref[...])
    for c in range(N_CT):
        cs = slice(c * LANES, (c + 1) * LANES)
        ucs = slice(C + c * LANES, C + (c + 1) * LANES)
        (xlo, xhi), (rlo, rhi) = _band_slices(c)
        z = jnp.dot(gu_ref[:, C + xlo:C + xhi].astype(BF16), wb_ref[c, rlo:rhi, :], preferred_element_type=F32)
        hin = hc_scr[:, cs]
        for r0 in range(0, tt, 2 * SUB):
            hs = []
            for r1 in (r0, r0 + SUB):
                a, b = _lru_ab(z[r1:r1 + SUB, :LANES] + gab_ref[:, cs], z[r1:r1 + SUB, LANES:] + gxb_ref[:, cs],
                               gu_ref[r1:r1 + SUB, ucs], sp[:, cs])
                for d, kp in zip((1, 2, 4), keep):
                    a_sh = jnp.where(kp, pltpu.roll(a, d, 0), 1.0)
                    b_sh = jnp.where(kp, pltpu.roll(b, d, 0), 0.0)
                    b = a * b_sh + b
                    a = a * a_sh
                h = b + a * hin
                hin = jnp.broadcast_to(h[SUB - 1:SUB, :], (SUB, LANES))
                hs.append(h)
            h2 = jnp.concatenate(hs, axis=0)
            hg_ref[r0:r0 + 2 * SUB, cs] = (h2 * gu_ref[r0:r0 + 2 * SUB, cs]).astype(hg_ref.dtype)
        hc_scr[:, cs] = hin

    @pl.when(t == nt - 1)
    def _():
        hl_ref[0] = hc_scr[0:1, :]


def lru_mix_prompt(gu, wb, gab, gxb, lam, *, batch, seq, tt):
    C = D_RNN
    nt = seq // tt
    row = lambda v: v.reshape(1, C)
    full2 = lambda shape: pl.BlockSpec(shape, lambda b, t: (0,) * len(shape))
    return pl.pallas_call(
        functools.partial(_lru_mix_prompt_kernel, tt=tt),
        grid=(batch, nt),
        in_specs=[
            pl.BlockSpec((tt, 2 * C), lambda b, t: (b * nt + t, 0)),
            full2((N_CT, 3 * LANES, 2 * LANES)), full2((1, C)), full2((1, C)), full2((1, C)),
        ],
        out_specs=[
            pl.BlockSpec((tt, C), lambda b, t: (b * nt + t, 0)),
            pl.BlockSpec((1, 1, C), lambda b, t: (b, 0, 0)),
        ],
        out_shape=[
            jax.ShapeDtypeStruct((batch * seq, C), BF16),
            jax.ShapeDtypeStruct((batch, 1, C), F32),
        ],
        scratch_shapes=[pltpu.VMEM((8, C), F32)],
        compiler_params=_cparams(("arbitrary", "arbitrary"), 48),
        name="lru_mix_prompt",
    )(gu, wb, row(gab), row(gxb), row(lam))


def _lru_mix_sample_kernel(gu_ref, cst_ref, hp_ref, cw_ref, cb_ref, wb_ref, gab_ref, gxb_ref, lam_ref,
                           hg_ref, cso_ref, hl_ref, cv_scr, xg_scr, *, db, s_len):
    del db
    C = D_RNN
    W1 = LRU_CONV_W - 1

    def ext(j, cs):
        if j < W1:
            return cst_ref[j, :, cs]
        return gu_ref[j - W1, :, C + cs.start:C + cs.stop]

    for c in range(N_CT):
        cs = slice(c * LANES, (c + 1) * LANES)
        for s in range(s_len):
            acc = cw_ref[0:1, cs] * ext(s, cs)
            for k in range(1, LRU_CONV_W):
                acc = acc + cw_ref[k:k + 1, cs] * ext(s + k, cs)
            acc = acc + cb_ref[:, cs]
            cv_scr[s, :, cs] = acc
            xg_scr[s, :, cs] = acc.astype(BF16)
        for k in range(W1):
            cso_ref[k, :, cs] = ext(s_len + k, cs)

    sp = -LRU_C * _softplus(-lam_ref[...])
    for c in range(N_CT):
        cs = slice(c * LANES, (c + 1) * LANES)
        h = hp_ref[:, cs]
        for s in range(s_len):
            a, b = _lru_gates(lambda lo, hi: xg_scr[s, :, lo:hi], wb_ref, c, cv_scr[s, :, cs],
                              gab_ref[:, cs], gxb_ref[:, cs], sp[:, cs])
            h = a * h + b
            hg_ref[s, :, cs] = h * gu_ref[s, :, cs]
        hl_ref[:, cs] = h


def lru_mix_sample(gu, cst, hp, cw, cb, wb, gab, gxb, lam, *, db, s_len):
    C = D_RNN
    row = lambda v: v.reshape(1, C)
    return pl.pallas_call(
        functools.partial(_lru_mix_sample_kernel, db=db, s_len=s_len),
        out_shape=[
            jax.ShapeDtypeStruct((s_len, db, C), F32),
            jax.ShapeDtypeStruct((LRU_CONV_W - 1, db, C), F32),
            jax.ShapeDtypeStruct((db, C), F32),
        ],
        scratch_shapes=[pltpu.VMEM((s_len, db, C), F32), pltpu.VMEM((s_len, db, C), BF16)],
        compiler_params=pltpu.CompilerParams(vmem_limit_bytes=48 * MIB),
        name="lru_mix_sample",
    )(gu, cst, hp, cw, row(cb), wb, row(gab), row(gxb), row(lam))


def _ln_silu(y, g, b):
    mu = jnp.mean(y, axis=-1, keepdims=True)
    yc = y - mu
    var = jnp.mean(yc * yc, axis=-1, keepdims=True)
    z = yc * lax.rsqrt(var + EPS) * g + b
    return z * _sigmoid(z)


def _cm_mix_prompt_kernel(u_ref, dw_ref, db_ref, lg_ref, lb_ref, o_ref, cs_ref, ubuf, cv_scr, *, tt):
    t = pl.program_id(1)
    nt = pl.num_programs(1)
    C = D_MODEL
    RB = 64
    HALO = 32
    OFF = HALO - (CM_KERNEL - 1)

    @pl.when(t == 0)
    def _():
        ubuf[0:HALO, :] = jnp.zeros((HALO, C), F32)

    ubuf[HALO:HALO + tt, :] = u_ref[...]

    WIN = RB + HALO
    for c in range(C // LANES):
        cs = slice(c * LANES, (c + 1) * LANES)
        for r0 in range(0, tt, RB):
            x = ubuf[r0:r0 + WIN, cs]
            acc = None
            for r in range(8):
                xr = x if r == 0 else pltpu.roll(x, WIN - r, 0)
                for o in range(OFF, OFF + CM_KERNEL):
                    if o % 8 != r:
                        continue
                    term = dw_ref[o - OFF:o - OFF + 1, cs] * xr[o - r:o - r + RB, :]
                    acc = term if acc is None else acc + term
            cv_scr[r0:r0 + RB, cs] = acc + db_ref[:, cs]

    @pl.when(t == nt - 1)
    def _():
        cs_ref[0] = ubuf[tt + OFF:tt + HALO, :]

    ubuf[0:HALO, :] = ubuf[tt:tt + HALO, :]

    for r0 in range(0, tt, RB):
        o_ref[r0:r0 + RB, :] = _ln_silu(cv_scr[r0:r0 + RB, :], lg_ref[...], lb_ref[...]).astype(o_ref.dtype)


def cm_mix_prompt(u, dw, db_, lg, lb, *, batch, seq, tt):
    C = D_MODEL
    nt = seq // tt
    row = lambda v: v.reshape(1, C)
    full2 = lambda shape: pl.BlockSpec(shape, lambda b, t: (0,) * len(shape))
    return pl.pallas_call(
        functools.partial(_cm_mix_prompt_kernel, tt=tt),
        grid=(batch, nt),
        in_specs=[
            pl.BlockSpec((tt, C), lambda b, t: (b * nt + t, 0)),
            full2((CM_KERNEL, C)), full2((1, C)), full2((1, C)), full2((1, C)),
        ],
        out_specs=[
            pl.BlockSpec((tt, C), lambda b, t: (b * nt + t, 0)),
            pl.BlockSpec((1, CM_KERNEL - 1, C), lambda b, t: (b, 0, 0)),
        ],
        out_shape=[
            jax.ShapeDtypeStruct((batch * seq, C), BF16),
            jax.ShapeDtypeStruct((batch, CM_KERNEL - 1, C), F32),
        ],
        scratch_shapes=[pltpu.VMEM((tt + 32, C), F32), pltpu.VMEM((tt, C), F32)],
        compiler_params=_cparams(("arbitrary", "arbitrary"), 48),
        name="cm_mix_prompt",
    )(u, dw, row(db_), row(lg), row(lb))


def _cm_mix_sample_kernel(u_ref, st_ref, dw_ref, db_ref, lg_ref, lb_ref, o_ref, so_ref, cv_scr, *, nb, s_len):
    del nb
    C = D_MODEL
    W1 = CM_KERNEL - 1

    def ext(j, cs):
        if j < W1:
            return st_ref[:, j, cs]
        return u_ref[j - W1, :, cs]

    for c in range(C // LANES):
        cs = slice(c * LANES, (c + 1) * LANES)
        for s in range(s_len):
            acc = dw_ref[0:1, cs] * ext(s, cs)
            for k in range(1, CM_KERNEL):
                acc = acc + dw_ref[k:k + 1, cs] * ext(s + k, cs)
            cv_scr[s, :, cs] = acc + db_ref[:, cs]
        for j in range(W1):
            so_ref[:, j, cs] = ext(j + s_len, cs)

    for s in range(s_len):
        o_ref[s] = _ln_silu(cv_scr[s], lg_ref[...], lb_ref[...])


def cm_mix_sample(u, st, dw, db_, lg, lb, *, db, s_len, nb):
    C = D_MODEL
    W1 = CM_KERNEL - 1
    row = lambda v: v.reshape(1, C)
    full1 = lambda shape: pl.BlockSpec(shape, lambda i: (0,) * len(shape))
    return pl.pallas_call(
        functools.partial(_cm_mix_sample_kernel, nb=nb, s_len=s_len),
        grid=(db // nb,),
        in_specs=[
            pl.BlockSpec((s_len, nb, C), lambda i: (0, i, 0)),
            pl.BlockSpec((nb, W1, C), lambda i: (i, 0, 0)),
            full1((CM_KERNEL, C)), full1((1, C)), full1((1, C)), full1((1, C)),
        ],
        out_specs=[
            pl.BlockSpec((s_len, nb, C), lambda i: (0, i, 0)),
            pl.BlockSpec((nb, W1, C), lambda i: (i, 0, 0)),
        ],
        out_shape=[
            jax.ShapeDtypeStruct((s_len, db, C), F32),
            jax.ShapeDtypeStruct((db, W1, C), F32),
        ],
        scratch_shapes=[pltpu.VMEM((s_len, nb, C), F32)],
        compiler_params=_cparams(("parallel",), 48),
        name="cm_mix_sample",
    )(u, st, dw, row(db_), row(lg), row(lb))


def _attn_prompt_kernel(q_ref, k_ref, v_ref, kp_ref, vp_ref, o_ref, l_ref, qs, kext, vext, os, ls, s_scr, *, ch):
    n = pl.program_id(2)
    qs[...] = q_ref[:, 0].reshape(ch, HD_ALL)
    kext[0:ATT_BLOCK, :] = kp_ref[:, 0].reshape(ATT_BLOCK, HD_ALL)
    kext[ATT_BLOCK:ATT_BLOCK + ch, :] = k_ref[:, 0].reshape(ch, HD_ALL)
    vext[0:ATT_BLOCK, :] = vp_ref[:, 0].reshape(ATT_BLOCK, HD_ALL)
    vext[ATT_BLOCK:ATT_BLOCK + ch, :] = v_ref[:, 0].reshape(ch, HD_ALL)
    qi = lax.broadcasted_iota(jnp.int32, (ATT_BLOCK, 2 * ATT_BLOCK), 0)
    kj = lax.broadcasted_iota(jnp.int32, (ATT_BLOCK, 2 * ATT_BLOCK), 1)
    band = jnp.logical_and(kj >= qi, kj <= qi + ATT_BLOCK)
    band_first = jnp.logical_and(band, jnp.logical_or(kj >= ATT_BLOCK, n > 0))
    lane = lax.broadcasted_iota(jnp.int32, (ATT_BLOCK, LANES), 1)
    dn = (((1,), (1,)), ((), ()))
    for i in range(ch // ATT_BLOCK):
        rows = slice(i * ATT_BLOCK, (i + 1) * ATT_BLOCK)
        krows = slice(i * ATT_BLOCK, (i + 2) * ATT_BLOCK)
        mask = band_first if i == 0 else band
        for h in range(ATT_HEADS):
            hs = slice(h * HEAD_DIM, (h + 1) * HEAD_DIM)
            s_scr[i % 2, h] = lax.dot_general(qs[rows, hs], kext[krows, hs], dn, preferred_element_type=F32)
        lse_tile = jnp.zeros((ATT_BLOCK, LANES), F32)
        for h in range(ATT_HEADS):
            hs = slice(h * HEAD_DIM, (h + 1) * HEAD_DIM)
            s = jnp.where(mask, s_scr[i % 2, h], -jnp.inf)
            m = jnp.max(jnp.maximum(s[:, :ATT_BLOCK], s[:, ATT_BLOCK:]), axis=-1, keepdims=True)
            p = jnp.exp2((s - m) * SCALE_LOG2E)
            l = jnp.sum(p[:, :ATT_BLOCK] + p[:, ATT_BLOCK:], axis=-1, keepdims=True)
            o = jnp.dot(p.astype(BF16), vext[krows, hs], preferred_element_type=F32)
            os[rows, hs] = (o * (1.0 / l)).astype(os.dtype)
            lse_tile = jnp.where(lane == h, m * SCALE + jnp.log(l), lse_tile)
        ls[rows, :] = lse_tile
    o_ref[:, 0] = os[...].reshape(o_ref.shape[0], o_ref.shape[2], HD_ALL)
    l_ref[:, 0] = ls[...].reshape(l_ref.shape[0], l_ref.shape[2], LANES)


def attn_prompt_group(qkv, g, *, batch, seq):
    win, dil = ATT_GROUPS[g]
    m = seq // dil
    ch = min(m, 512)
    nch = m // ch
    bpc = ch // ATT_BLOCK
    nblk = m // ATT_BLOCK
    rpt = QKV_TILE // dil
    ntc = ch // rpt
    prpt = min(rpt, ATT_BLOCK)
    ntiles = batch * seq // QKV_TILE
    cur_view = qkv.reshape(ntiles, dil, rpt, QKV_COLS)
    prev_view = qkv.reshape(ntiles * (rpt // prpt), dil, prpt, QKV_COLS)
    ncb = QKV_COLS // HD_ALL
    cq, ck, cv = 3 * g, 3 * g + 1, 3 * g + 2

    def cur(cb):
        return pl.BlockSpec((ntc, 1, rpt, HD_ALL), lambda b, r, n: (b * nch + n, r, 0, cb))

    def prev(cb):
        return pl.BlockSpec((ATT_BLOCK // prpt, 1, prpt, HD_ALL),
                            lambda b, r, n: (b * nblk + jnp.maximum(n * bpc - 1, 0), r, 0, cb))

    o, l = pl.pallas_call(
        functools.partial(_attn_prompt_kernel, ch=ch),
        grid=(batch, dil, nch),
        in_specs=[cur(cq), cur(ck), cur(cv), prev(ck), prev(cv)],
        out_specs=[pl.BlockSpec((ntc, 1, rpt, HD_ALL), lambda b, r, n: (b * nch + n, r, 0, 0)),
                   pl.BlockSpec((ntc, 1, rpt, LANES), lambda b, r, n: (b * nch + n, r, 0, 0))],
        out_shape=[jax.ShapeDtypeStruct((ntiles, dil, rpt, HD_ALL), BF16),
                   jax.ShapeDtypeStruct((ntiles, dil, rpt, LANES), F32)],
        scratch_shapes=[pltpu.VMEM((ch, HD_ALL), BF16),
                        pltpu.VMEM((ATT_BLOCK + ch, HD_ALL), BF16), pltpu.VMEM((ATT_BLOCK + ch, HD_ALL), BF16),
                        pltpu.VMEM((ch, HD_ALL), BF16), pltpu.VMEM((ch, LANES), F32),
                        pltpu.VMEM((2, ATT_HEADS, ATT_BLOCK, 2 * ATT_BLOCK), F32)],
        compiler_params=_cparams(("parallel", "parallel", "arbitrary"), 48),
        name=f"attn_prompt_g{g}",
    )(cur_view, cur_view, cur_view, prev_view, prev_view)
    return o.reshape(batch * seq, HD_ALL), l.reshape(batch * seq, LANES)


QKV_TILES = QKV_COLS // LANES
GROUP_TILES = GROUP_COLS // LANES


def _attn_sample_kernel(qkv_ref, c0_ref, c1_ref, c2_ref, o_ref, s_scr, p_scr, *, s_len):
    nk = ATT_BLOCK * ATT_HEADS
    caches = (c0_ref, c1_ref, c2_ref)
    col = lax.broadcasted_iota(jnp.int32, (ATT_HEADS, nk), 1)
    rowh = lax.broadcasted_iota(jnp.int32, (ATT_HEADS, nk), 0)
    own = (col & (ATT_HEADS - 1)) == rowh
    key = col >> 3
    dn = (((1,), (1,)), ((), ()))

    def kvflat(g, s, which):
        r = 0 if g == 0 else s
        return caches[g][0, :, r, which].reshape(nk, HEAD_DIM).astype(BF16)

    def tile(s, g, which):
        lo = g * GROUP_TILES + which * ATT_HEADS
        return qkv_ref[0, s, lo:lo + ATT_HEADS, :]

    def rnd(x):
        return x.astype(BF16).astype(F32)

    q0 = qkv_ref[0, :, 0:ATT_HEADS, :].reshape(s_len * ATT_HEADS, HEAD_DIM).astype(BF16)
    s0 = lax.dot_general(q0, kvflat(0, 0, 0), dn, preferred_element_type=F32)
    s_scr[0:s_len] = s0.reshape(s_len, ATT_HEADS, nk)
    for g in range(1, N_GROUPS):
        for s in range(s_len):
            s_scr[g * s_len + s] = lax.dot_general(tile(s, g, 0).astype(BF16), kvflat(g, s, 0), dn,
                                                   preferred_element_type=F32)

    stats = {}
    for g in range(N_GROUPS):
        for s in range(s_len):
            idx = g * s_len + s
            valid = jnp.logical_and(own, key >= s) if g == 0 else own
            sc = jnp.where(valid, s_scr[idx], -jnp.inf)
            news = tuple(range(s + 1)) if g == 0 else (s,)
            qf = rnd(tile(s, g, 0))
            sn = [jnp.sum(qf * rnd(tile(j, g, 1)), axis=-1, keepdims=True) for j in news]
            m = jnp.max(sc, axis=-1, keepdims=True)
            for x in sn:
                m = jnp.maximum(m, x)
            p = jnp.exp2((sc - m) * SCALE_LOG2E)
            pn = [jnp.exp2((x - m) * SCALE_LOG2E) for x in sn]
            l = jnp.sum(p, axis=-1, keepdims=True)
            for x in pn:
                l = l + x
            p_scr[idx] = p
            stats[(g, s)] = (m, l, pn, news)

    outs = {}
    p0 = p_scr[0:s_len].reshape(s_len * ATT_HEADS, nk).astype(BF16)
    o0 = jnp.dot(p0, kvflat(0, 0, 1), preferred_element_type=F32).reshape(s_len, ATT_HEADS, HEAD_DIM)
    for g in range(N_GROUPS):
        for s in range(s_len):
            idx = g * s_len + s
            m, l, pn, news = stats[(g, s)]
            if g == 0:
                o = o0[s]
            else:
                o = jnp.dot(p_scr[idx].astype(BF16), kvflat(g, s, 1), preferred_element_type=F32)
            for x, j in zip(pn, news):
                o = o + rnd(x) * rnd(tile(j, g, 2))
            outs[(g, s)] = (o * (1.0 / l), m * SCALE + jnp.log(l))

    for s in range(s_len):
        lses = [outs[(g, s)][1] for g in range(N_GROUPS)]
        mx = jnp.maximum(jnp.maximum(lses[0], lses[1]), lses[2])
        es = [jnp.exp(x - mx) for x in lses]
        inv = 1.0 / (es[0] + es[1] + es[2])
        o_ref[0, s] = ((es[0] * inv) * outs[(0, s)][0] + (es[1] * inv) * outs[(1, s)][0]
                       + (es[2] * inv) * outs[(2, s)][0])


def attn_sample(qkv, c0, c1, c2, layer, *, db, s_len):
    nk = ATT_BLOCK * ATT_HEADS

    def cspec(c):
        r = min(c.shape[2], s_len)
        return pl.BlockSpec((1, ATT_BLOCK, r, 2, ATT_HEADS, HEAD_DIM), lambda b: (layer * db + b, 0, 0, 0, 0, 0))

    return pl.pallas_call(
        functools.partial(_attn_sample_kernel, s_len=s_len),
        grid=(db,),
        in_specs=[pl.BlockSpec((1, s_len, QKV_TILES, LANES), lambda b: (b, 0, 0, 0)),
                  cspec(c0), cspec(c1), cspec(c2)],
        out_specs=pl.BlockSpec((1, s_len, ATT_HEADS, HEAD_DIM), lambda b: (b, 0, 0, 0)),
        out_shape=jax.ShapeDtypeStruct((db, s_len, ATT_HEADS, HEAD_DIM), F32),
        scratch_shapes=[pltpu.VMEM((N_GROUPS * s_len, ATT_HEADS, nk), F32),
                        pltpu.VMEM((N_GROUPS * s_len, ATT_HEADS, nk), F32)],
        compiler_params=_cparams(("arbitrary",), 56),
        name="attn_sample",
    )(qkv, c0, c1, c2)


def _gate_band(ga_w, gx_w):
    eye = jnp.eye(LRU_BLOCKS, dtype=BF16)

    def dense(w):
        d = jnp.einsum('ncd,nm->ncmd', w.astype(BF16), eye).reshape(D_RNN, D_RNN)
        return jnp.pad(d, ((LANES, LANES), (0, 0)))

    da, dx = dense(ga_w), dense(gx_w)
    bands = [jnp.concatenate([da[c * LANES:(c + 3) * LANES, c * LANES:(c + 1) * LANES],
                              dx[c * LANES:(c + 3) * LANES, c * LANES:(c + 1) * LANES]], axis=1)
             for c in range(N_CT)]
    return jnp.stack(bands)


def _tile_order(dil):
    rows = np.arange(QKV_TILE)
    rpt = QKV_TILE // dil
    return (rows % rpt) * dil + rows // rpt


def _perm_matrices():
    mats = []
    for _, dil in ATT_GROUPS[1:]:
        p = np.zeros((QKV_TILE, QKV_TILE), np.float32)
        p[np.arange(QKV_TILE), _tile_order(dil)] = 1.0
        mats.append(p)
    return np.stack(mats)


def _rope_tables(pos):
    half = HEAD_DIM // 2
    inv_freq = ROPE_THETA ** (-jnp.arange(half, dtype=F32) / half)
    ang = pos.astype(F32)[:, None] * inv_freq
    cos, sin = jnp.cos(ang), jnp.sin(ang)
    return jnp.concatenate([cos, cos], axis=-1), jnp.concatenate([-sin, sin], axis=-1)


def _row_tile(n, cap):
    t = cap
    while n % t:
        t //= 2
    return t


def kernel(x_prompt, x_sample, state_lru_conv, state_lru_h, state_cm_conv, cache_kv_w128, cache_kv_w512, cache_kv_w2048, norm_mix, norm_ffn, norm_final, lru_w_in, lru_conv_w, lru_conv_b, lru_gate_a_w, lru_gate_a_b, lru_gate_x_w, lru_gate_x_b, lru_lambda, lru_w_out, cm_w_pw1, cm_b_pw1, cm_dw_w, cm_dw_b, cm_ln_g, cm_ln_b, cm_w_pw2, cm_b_pw2, att_w_qkv, att_w_o, ffn_w1, ffn_w2):
    B, T, D = x_prompt.shape
    DB, S, _ = x_sample.shape
    depth = norm_mix.shape[0]
    NP, NS = B * T, DB * S
    assert D == D_MODEL and T % (ATT_GROUPS[-1][1] * ATT_BLOCK) == 0 and T % QKV_TILE == 0 and S == 4
    for c, (win, _) in zip((cache_kv_w128, cache_kv_w512, cache_kv_w2048), ATT_GROUPS):
        assert c.shape[2] == win
    tmp = _row_tile(NP, 1024)
    tms = _row_tile(NS, 512)
    zeros_d = jnp.zeros((D,), F32)

    xp = x_prompt.reshape(NP, D)
    xs = jnp.swapaxes(x_sample, 0, 1).reshape(NS, D)

    pos_p = np.arange(T).reshape(T // QKV_TILE, QKV_TILE)
    pos_g = np.stack([pos_p[:, _tile_order(dil)].reshape(T) for _, dil in ATT_GROUPS])
    cos_p, sin_p = _rope_tables(jnp.asarray(pos_g.reshape(-1)))
    cos_p, sin_p = cos_p.reshape(N_GROUPS, T, HEAD_DIM), sin_p.reshape(N_GROUPS, T, HEAD_DIM)
    pm = _perm_matrices()
    perms = jnp.asarray(pm, BF16)
    perms_t = jnp.asarray(np.swapaxes(pm, 1, 2), BF16)
    cos_s, sin_s = _rope_tables(PAST_LEN + jnp.arange(NS) // DB)

    ffn_w1b, ffn_w2b = ffn_w1.astype(BF16), ffn_w2.astype(BF16)

    p_lru_conv, p_lru_h, s_lru_conv, s_lru_h = [], [], [], []
    p_cm_conv, s_cm_conv = [], []
    p_kv = [[] for _ in ATT_GROUPS]
    s_kv = [[] for _ in ATT_GROUPS]

    for i in range(depth):
        m, j = i % 3, i // 3
        if m == 0:
            w_in = lru_w_in[j].astype(BF16)
            w_out = lru_w_out[j].astype(BF16)
            wb = _gate_band(lru_gate_a_w[j], lru_gate_x_w[j])
            lp = (lru_conv_w[j], lru_conv_b[j], wb, lru_gate_a_b[j], lru_gate_x_b[j], lru_lambda[j])
            gu, cp = norm_proj_gelu(xp, norm_mix[i], w_in, (lru_conv_w[j], lru_conv_b[j], T), tm=_row_tile(T, 512))
            hg, hp = lru_mix_prompt(gu, wb, lru_gate_a_b[j], lru_gate_x_b[j], lru_lambda[j], batch=B, seq=T, tt=512)
            xp = out_proj(hg, w_out, zeros_d, xp, tm=tmp)
            p_lru_conv.append(cp)
            p_lru_h.append(hp.reshape(B, D_RNN))
            gus = norm_proj_gelu(xs, norm_mix[i], w_in, tm=tms)
            hgs, cs, hs = lru_mix_sample(gus.reshape(S, DB, 2 * D_RNN), jnp.swapaxes(state_lru_conv[j], 0, 1),
                                         state_lru_h[j], *lp, db=DB, s_len=S)
            xs = out_proj(hgs.reshape(NS, D_RNN), w_out, zeros_d, xs, tm=tms)
            s_lru_conv.append(jnp.swapaxes(cs, 0, 1))
            s_lru_h.append(hs)
        elif m == 1:
            w1 = cm_w_pw1[j].astype(BF16)
            w2 = cm_w_pw2[j].astype(BF16)
            cp_ = (cm_dw_w[j], cm_dw_b[j], cm_ln_g[j], cm_ln_b[j])
            u = norm_glu(xp, norm_mix[i], w1, cm_b_pw1[j], tm=tmp, tn=1024)
            y, cp = cm_mix_prompt(u, *cp_, batch=B, seq=T, tt=512)
            xp = out_proj(y, w2, cm_b_pw2[j], xp, tm=tmp)
            p_cm_conv.append(cp)
            us = norm_glu(xs, norm_mix[i], w1, cm_b_pw1[j], tm=tms, tn=512)
            nb = _row_tile(DB, 32)
            ys, cs = cm_mix_sample(us.reshape(S, DB, D), state_cm_conv[j], *cp_,
                                   db=DB, s_len=S, nb=nb)
            xs = out_proj(ys.reshape(NS, D), w2, cm_b_pw2[j], xs, tm=tms)
            s_cm_conv.append(cs)
        else:
            wqkv = att_w_qkv[j].astype(BF16)
            wo = att_w_o[j].astype(BF16)
            qkv = norm_qkv(xp, norm_mix[i], wqkv, cos_p, sin_p, perms, tm=QKV_TILE, out_dtype=BF16)
            og = [attn_prompt_group(qkv, g, batch=B, seq=T) for g in range(N_GROUPS)]
            xp = merge_out_proj([o for o, _ in og], [l for _, l in og], perms_t, wo, xp, tm=QKV_TILE)
            for g, (win, dil) in enumerate(ATT_GROUPS):
                keep = min(win, T)
                kvg = qkv.reshape(B, T, QKV_COLS)[:, T - keep:, g * GROUP_COLS + HD_ALL:(g + 1) * GROUP_COLS]
                if dil > 1:
                    kvg = kvg.reshape(B, keep // QKV_TILE, dil, QKV_TILE // dil, 2 * HD_ALL)
                    kvg = jnp.swapaxes(kvg, 2, 3)
                p_kv[g].append(kvg.astype(F32).reshape(B, keep, 2, ATT_HEADS, HEAD_DIM))
            qkvs = norm_qkv(xs, norm_mix[i], wqkv, cos_s[None], sin_s[None], None, tm=tms, out_dtype=F32)
            qkvs3 = jnp.swapaxes(qkvs.reshape(S, DB, QKV_COLS), 0, 1)
            nl = cache_kv_w128.shape[0]
            caches = [c.reshape(nl * DB, ATT_BLOCK, c.shape[2] // ATT_BLOCK, 2, ATT_HEADS, HEAD_DIM)
                      for c in (cache_kv_w128, cache_kv_w512, cache_kv_w2048)]
            os_ = attn_sample(qkvs3.reshape(DB, S, QKV_TILES, LANES), *caches, j, db=DB, s_len=S)
            xs = out_proj(jnp.swapaxes(os_.reshape(DB, S, HD_ALL), 0, 1).reshape(NS, HD_ALL), wo, zeros_d, xs, tm=tms)
            for g in range(N_GROUPS):
                kvg = qkvs3[:, :, g * GROUP_COLS + HD_ALL:(g + 1) * GROUP_COLS]
                s_kv[g].append(kvg.reshape(DB, S, 2, ATT_HEADS, HEAD_DIM))
        final = i == depth - 1
        xp = ffn(xp, norm_ffn[i], ffn_w1b, ffn_w2b, norm_final, i, tm=tmp, tf=1024, final=final)
        xs = ffn(xs, norm_ffn[i], ffn_w1b, ffn_w2b, norm_final, i, tm=tms, tf=1024, final=final)

    return (xp.reshape(B, T, D), jnp.swapaxes(xs.reshape(S, DB, D), 0, 1),
            jnp.stack(p_lru_conv), jnp.stack(p_lru_h), jnp.stack(p_cm_conv),
            jnp.stack(p_kv[0]), jnp.stack(p_kv[1]), jnp.stack(p_kv[2]),
            jnp.stack(s_lru_conv), jnp.stack(s_lru_h), jnp.stack(s_cm_conv),
            jnp.stack(s_kv[0]), jnp.stack(s_kv[1]), jnp.stack(s_kv[2]))
```
